```python
import math
import jax, jax.numpy as jnp
from jax import lax
import numpy as np

D_MODEL = 1024
BATCH = 16
SEQ = 2048
DEPTH = 4

N_A = DEPTH // 2
N_B = DEPTH - N_A
CONV_WIDTH = 31
FFN_CONV_WIDTH = 3
D_FF = 2816
N_HEADS = 8
HEAD_DIM = D_MODEL // N_HEADS
WINDOWS = (128, 512, 2048)
DILATIONS = (1, 4, 16)
N_GROUPS = len(WINDOWS)
Q_WIDTH = N_GROUPS * N_HEADS * HEAD_DIM
BLOCK = 128
EPS = 1e-6
NEG_INF = -1e30

kernel_name = "yoco_conformer_dilated_hybrid"


def rms_norm(x, g):
    xf = x.astype(jnp.float32)
    y = xf * lax.rsqrt(jnp.mean(xf * xf, axis=-1, keepdims=True) + EPS)
    return (y * g.astype(jnp.float32)).astype(x.dtype)


def layer_norm(x, g, b):
    xf = x.astype(jnp.float32)
    mu = jnp.mean(xf, axis=-1, keepdims=True)
    var = jnp.mean(jnp.square(xf - mu), axis=-1, keepdims=True)
    y = (xf - mu) * lax.rsqrt(var + EPS)
    return (y * g.astype(jnp.float32) + b.astype(jnp.float32)).astype(x.dtype)


def causal_dwconv(x, w, b):
    k, c = w.shape
    y = lax.conv_general_dilated(
        x, w[:, None, :].astype(x.dtype), window_strides=(1,), padding=[(k - 1, 0)],
        dimension_numbers=("NWC", "WIO", "NWC"), feature_group_count=c)
    return y + b


def conformer_conv_module(h, w_in, b_in, dw, dw_b, ln_g, ln_b, w_out, b_out):
    u = h @ w_in + b_in
    u = u[..., :D_MODEL] * jax.nn.sigmoid(u[..., D_MODEL:])
    u = causal_dwconv(u, dw, dw_b)
    u = layer_norm(u, ln_g, ln_b)
    u = jax.nn.silu(u)
    return u @ w_out + b_out


def conv_ffn(h, w_in, dw, dw_b, w_out):
    u = causal_dwconv(h @ w_in, dw, dw_b)
    a, g = u[..., :D_FF], u[..., D_FF:]
    return (jax.nn.silu(g) * a) @ w_out


def dilated_branch(q, k, v, window, dil):
    b, s, h, dh = q.shape
    steps = window // dil
    assert steps <= BLOCK and s % dil == 0
    L = s // dil
    nblk = -(-L // BLOCK)
    lp = nblk * BLOCK

    def to_sub(t):
        t = t.reshape(b, L, dil, h, dh).transpose(0, 2, 3, 1, 4)
        return jnp.pad(t, ((0, 0), (0, 0), (0, 0), (0, lp - L), (0, 0)))

    qs, ks, vs = to_sub(q), to_sub(k), to_sub(v)
    q_blk = qs.reshape(b, dil, h, nblk, BLOCK, dh)

    def band(t):
        tp = jnp.pad(t, ((0, 0), (0, 0), (0, 0), (BLOCK, 0), (0, 0)))
        prev = tp[:, :, :, :lp].reshape(b, dil, h, nblk, BLOCK, dh)
        cur = tp[:, :, :, BLOCK:].reshape(b, dil, h, nblk, BLOCK, dh)
        return jnp.concatenate([prev, cur], axis=-2)

    k_band, v_band = band(ks), band(vs)
    scores = jnp.einsum("brhnqc,brhnkc->brhnqk", q_blk, k_band).astype(jnp.float32)
    scores = scores * (1.0 / math.sqrt(dh))

    blk_i = jnp.arange(nblk)[:, None, None]
    qi = jnp.arange(BLOCK)[None, :, None]
    kk = jnp.arange(2 * BLOCK)[None, None, :]
    dist = qi + BLOCK - kk
    key_pos = blk_i * BLOCK + kk - BLOCK
    valid = (dist >= 0) & (dist <= steps) & (key_pos >= 0)
    scores = jnp.where(valid, scores, NEG_INF)

    m = jnp.max(scores, axis=-1, keepdims=True)
    p = jnp.exp(scores - m)
    den = jnp.sum(p, axis=-1, keepdims=True)
    out = jnp.einsum("brhnqk,brhnkc->brhnqc", p.astype(v.dtype), v_band)
    out = out / den.astype(out.dtype)
    lse = (m + jnp.log(den))[..., 0]

    out = out.reshape(b, dil, h, lp, dh)[:, :, :, :L]
    out = out.transpose(0, 3, 1, 2, 4).reshape(b, s, h, dh)
    lse = lse.reshape(b, dil, h, lp)[:, :, :, :L]
    lse = lse.transpose(0, 3, 1, 2).reshape(b, s, h)
    return out, lse


def dilated_mixture_attention(h, w_q, w_o, k, v):
    b, s, _ = h.shape
    q = (h @ w_q).reshape(b, s, N_GROUPS, N_HEADS, HEAD_DIM)
    outs, lses = [], []
    for g in range(N_GROUPS):
        o, l = dilated_branch(q[:, :, g], k[:, :, g], v[:, :, g], WINDOWS[g], DILATIONS[g])
        outs.append(o)
        lses.append(l)
    outs = jnp.stack(outs, axis=0)
    wts = jax.nn.softmax(jnp.stack(lses, axis=0), axis=0)
    merged = jnp.sum(wts[..., None].astype(outs.dtype) * outs, axis=0)
    return merged.reshape(b, s, N_HEADS * HEAD_DIM) @ w_o


def setup_inputs(seed: int = 0) -> dict:
    key = jax.random.key(seed)
    ks = iter(jax.random.split(key, 32))

    def nrm(shape, fan_in):
        return jax.random.normal(next(ks), shape, jnp.float32) * (fan_in ** -0.5)

    def gain(shape):
        return 1.0 + 0.02 * jax.random.normal(next(ks), shape, jnp.float32)

    def bias(shape):
        return 0.01 * jax.random.normal(next(ks), shape, jnp.float32)

    d = D_MODEL
    return {
        "x": jax.random.normal(next(ks), (BATCH, SEQ, d), jnp.float32),
        "mix_pre_g": gain((DEPTH, d)),
        "mix_post_g": gain((DEPTH, d)),
        "ffn_pre_g": gain((DEPTH, d)),
        "ffn_post_g": gain((DEPTH, d)),
        "cm_w_in": nrm((N_A, d, 2 * d), d),
        "cm_b_in": bias((N_A, 2 * d)),
        "cm_dw": nrm((N_A, CONV_WIDTH, d), CONV_WIDTH),
        "cm_dw_b": bias((N_A, d)),
        "cm_ln_g": gain((N_A, d)),
        "cm_ln_b": bias((N_A, d)),
        "cm_w_out": nrm((N_A, d, d), d),
        "cm_b_out": bias((N_A, d)),
        "kv_norm_g": gain((d,)),
        "w_kv": nrm((d, 2 * Q_WIDTH), d),
        "w_q": nrm((N_B, d, Q_WIDTH), d),
        "w_o": nrm((N_B, N_HEADS * HEAD_DIM, d), N_HEADS * HEAD_DIM),
        "ffn_w_in": nrm((DEPTH, d, 2 * D_FF), d),
        "ffn_dw": nrm((DEPTH, FFN_CONV_WIDTH, 2 * D_FF), FFN_CONV_WIDTH),
        "ffn_dw_b": bias((DEPTH, 2 * D_FF)),
        "ffn_w_out": nrm((DEPTH, D_FF, d), D_FF),
    }


def reference(x, mix_pre_g, mix_post_g, ffn_pre_g, ffn_post_g,
              cm_w_in, cm_b_in, cm_dw, cm_dw_b, cm_ln_g, cm_ln_b, cm_w_out, cm_b_out,
              kv_norm_g, w_kv, w_q, w_o,
              ffn_w_in, ffn_dw, ffn_dw_b, ffn_w_out):
    b, s, _ = x.shape
    k_sh = v_sh = None
    for i in range(DEPTH):
        h = rms_norm(x, mix_pre_g[i])
        if i < N_A:
            y = conformer_conv_module(h, cm_w_in[i], cm_b_in[i], cm_dw[i], cm_dw_b[i],
                                      cm_ln_g[i], cm_ln_b[i], cm_w_out[i], cm_b_out[i])
        else:
            j = i - N_A
            y = dilated_mixture_attention(h, w_q[j], w_o[j], k_sh, v_sh)
        x = x + rms_norm(y, mix_post_g[i])
        h = rms_norm(x, ffn_pre_g[i])
        y = conv_ffn(h, ffn_w_in[i], ffn_dw[i], ffn_dw_b[i], ffn_w_out[i])
        x = x + rms_norm(y, ffn_post_g[i])
        if i == N_A - 1:
            kv = (rms_norm(x, kv_norm_g) @ w_kv).reshape(b, s, 2, N_GROUPS, N_HEADS, HEAD_DIM)
            k_sh, v_sh = kv[:, :, 0], kv[:, :, 1]
    return x
```

```python
import functools
import math

import jax
import jax.numpy as jnp
from jax import lax
from jax.experimental import pallas as pl
from jax.experimental.pallas import tpu as pltpu

D = 1024
F = 2816
H = 8
DH = 128
DILS = (1, 4, 16)
BLK = 128
KW = 31
EPS = 1e-6
NEG = -1e30

T = 512
FC = 256
NC = F // FC
HALO = 32
RB = 64
VMEM_LIMIT = 56 * 1024 * 1024

F32 = jnp.float32
BF16 = jnp.bfloat16


def _rms(x, g):
    return x * lax.rsqrt(jnp.mean(x * x, axis=-1, keepdims=True) + EPS) * g


def _const_spec(shape):
    n = len(shape)
    return pl.BlockSpec(shape, lambda *_: (0,) * n, pipeline_mode=pl.Buffered(1))


def _ffn_kernel(x_ref, gpre_ref, win_ref, dwc_ref, wout_ref, gpost_ref, o_ref,
                h_ref, ubuf_ref, act_ref, acc_ref, tail_ref):
    s = pl.program_id(1)

    @pl.when(s == 0)
    def _():
        tail_ref[...] = jnp.zeros_like(tail_ref)

    h_ref[...] = _rms(x_ref[...], gpre_ref[...]).astype(BF16)
    acc_ref[...] = jnp.zeros_like(acc_ref)

    def chunk(c, carry):
        u = jnp.dot(h_ref[...], win_ref[c], preferred_element_type=F32)
        ubuf_ref[0:8, :] = tail_ref[c]
        ubuf_ref[8:8 + T, :] = u
        tail_ref[c] = ubuf_ref[T:T + 8, :]
        w = dwc_ref[c]
        w0, w1, w2, wb = w[0:1], w[1:2], w[2:3], w[3:4]
        for rb in range(T // RB):
            r0 = rb * RB
            y = (w0 * ubuf_ref[6 + r0:6 + r0 + RB, :]
                 + w1 * ubuf_ref[7 + r0:7 + r0 + RB, :]
                 + w2 * ubuf_ref[8 + r0:8 + r0 + RB, :]) + wb
            a = y[:, :FC]
            g = y[:, FC:]
            act_ref[r0:r0 + RB, :] = ((g / (1.0 + jnp.exp(-g))) * a).astype(BF16)
        acc_ref[...] += jnp.dot(act_ref[...], wout_ref[c], preferred_element_type=F32)
        return carry

    lax.fori_loop(0, NC, chunk, 0)
    o_ref[...] = x_ref[...] + _rms(acc_ref[...], gpost_ref[...])


def _ffn(x, gpre, w_in, dw, dwb, w_out, gpost):
    b, s, _ = x.shape
    win_c = jnp.concatenate(
        [w_in[:, :F].reshape(D, NC, FC), w_in[:, F:].reshape(D, NC, FC)], axis=2
    ).transpose(1, 0, 2).astype(BF16)
    taps = jnp.concatenate([dw, dwb[None, :], jnp.zeros((4, 2 * F), F32)], axis=0)
    dwc = jnp.concatenate(
        [taps[:, :F].reshape(8, NC, FC), taps[:, F:].reshape(8, NC, FC)], axis=2
    ).transpose(1, 0, 2)
    wout_c = w_out.reshape(NC, FC, D).astype(BF16)
    return pl.pallas_call(
        _ffn_kernel,
        out_shape=jax.ShapeDtypeStruct(x.shape, F32),
        grid=(b, s // T),
        in_specs=[
            pl.BlockSpec((None, T, D), lambda i, j: (i, j, 0)),
            _const_spec((1, D)),
            _const_spec((NC, D, 2 * FC)),
            _const_spec((NC, 8, 2 * FC)),
            _const_spec((NC, FC, D)),
            _const_spec((1, D)),
        ],
        out_specs=pl.BlockSpec((None, T, D), lambda i, j: (i, j, 0)),
        scratch_shapes=[
            pltpu.VMEM((T, D), BF16),
            pltpu.VMEM((T + 8, 2 * FC), F32),
            pltpu.VMEM((T, FC), BF16),
            pltpu.VMEM((T, D), F32),
            pltpu.VMEM((NC, 8, 2 * FC), F32),
        ],
        compiler_params=pltpu.CompilerParams(
            dimension_semantics=("arbitrary", "arbitrary"),
            vmem_limit_bytes=VMEM_LIMIT),
        name="conv_ffn",
    )(x, gpre.reshape(1, D), win_c, dwc, wout_c, gpost.reshape(1, D))


def _conf_kernel(x_ref, gpre_ref, win_ref, bin_ref, dw_ref, dwb_ref, lng_ref, lnb_ref,
                 wout_ref, bout_ref, gpost_ref, o_ref, h_ref, gbuf_ref, cbuf_ref, sbuf_ref):
    s = pl.program_id(1)

    @pl.when(s == 0)
    def _():
        gbuf_ref[0:HALO, :] = jnp.zeros((HALO, D), F32)

    @pl.when(s > 0)
    def _():
        gbuf_ref[0:HALO, :] = gbuf_ref[T:T + HALO, :]

    h_ref[...] = _rms(x_ref[...], gpre_ref[...]).astype(BF16)
    CB = 256
    for cb in range(D // CB):
        c0 = cb * CB
        ua = jnp.dot(h_ref[...], win_ref[:, c0:c0 + CB], preferred_element_type=F32)
        ug = jnp.dot(h_ref[...], win_ref[:, D + c0:D + c0 + CB], preferred_element_type=F32)
        ua = ua + bin_ref[:, c0:c0 + CB]
        ug = ug + bin_ref[:, D + c0:D + c0 + CB]
        gbuf_ref[HALO:HALO + T, c0:c0 + CB] = ua * (1.0 / (1.0 + jnp.exp(-ug)))

    off = HALO - (KW - 1)
    for lb in range(D // 128):
        l0 = lb * 128

        for rb in range(T // RB):
            r0 = rb * RB
            acc = jnp.broadcast_to(dwb_ref[:, l0:l0 + 128], (RB, 128))
            for sh in range(8):
                taps = [(a, 8 * a + sh - off) for a in range(HALO // 8 + 1)
                        if 0 <= 8 * a + sh - off < KW]
                rows = 8 * taps[-1][0] + RB
                shifted = gbuf_ref[r0 + sh:r0 + sh + rows, l0:l0 + 128]
                for a, j in taps:
                    acc = acc + dw_ref[j:j + 1, l0:l0 + 128] * shifted[8 * a:8 * a + RB]
            cbuf_ref[r0:r0 + RB, l0:l0 + 128] = acc

    def ln_rows(rb, carry):
        r0 = pl.multiple_of(rb * RB, RB)
        c = cbuf_ref[pl.ds(r0, RB), :]
        mu = jnp.mean(c, axis=-1, keepdims=True)
        cc = c - mu
        var = jnp.mean(cc * cc, axis=-1, keepdims=True)
        y = cc * lax.rsqrt(var + EPS) * lng_ref[...] + lnb_ref[...]
        sbuf_ref[pl.ds(r0, RB), :] = (y * (1.0 / (1.0 + jnp.exp(-y)))).astype(BF16)
        return carry

    lax.fori_loop(0, T // RB, ln_rows, 0)

    y = jnp.dot(sbuf_ref[...], wout_ref[...], preferred_element_type=F32) + bout_ref[...]
    o_ref[...] = x_ref[...] + _rms(y, gpost_ref[...])


def _conformer(x, gpre, w_in, b_in, dw, dwb, lng, lnb, w_out, b_out, gpost):
    b, s, _ = x.shape
    dwp = jnp.concatenate([dw, jnp.zeros((32 - KW, D), F32)], axis=0)
    return pl.pallas_call(
        _conf_kernel,
        out_shape=jax.ShapeDtypeStruct(x.shape, F32),
        grid=(b, s // T),
        in_specs=[
            pl.BlockSpec((None, T, D), lambda i, j: (i, j, 0)),
            _const_spec((1, D)),
            _const_spec((D, 2 * D)),
            _const_spec((1, 2 * D)),
            _const_spec((32, D)),
            _const_spec((1, D)),
            _const_spec((1, D)),
            _const_spec((1, D)),
            _const_spec((D, D)),
            _const_spec((1, D)),
            _const_spec((1, D)),
        ],
        out_specs=pl.BlockSpec((None, T, D), lambda i, j: (i, j, 0)),
        scratch_shapes=[
            pltpu.VMEM((T, D), BF16),
            pltpu.VMEM((T + HALO, D), F32),
            pltpu.VMEM((T, D), F32),
            pltpu.VMEM((T, D), BF16),
        ],
        compiler_params=pltpu.CompilerParams(
            dimension_semantics=("arbitrary", "arbitrary"),
            vmem_limit_bytes=VMEM_LIMIT),
        name="conformer",
    )(x, gpre.reshape(1, D), w_in.astype(BF16), b_in.reshape(1, 2 * D), dwp, dwb.reshape(1, D),
      lng.reshape(1, D), lnb.reshape(1, D), w_out.astype(BF16), b_out.reshape(1, D),
      gpost.reshape(1, D))


def _proj_kernel(*refs, n_in, lt):
    x_refs = refs[:n_in]
    g_ref, w_ref, o_ref, h_ref = refs[n_in:]
    for i in range(n_in):
        h_ref[i * lt:(i + 1) * lt, :] = _rms(x_refs[i][...], g_ref[...]).astype(BF16)
    n = w_ref.shape[1]
    NB = 512
    for nb in range(n // NB):
        o_ref[:, nb * NB:(nb + 1) * NB] = jnp.dot(
            h_ref[...], w_ref[:, nb * NB:(nb + 1) * NB], preferred_element_type=F32).astype(BF16)


def _proj(x, g, w, dil):
    b, s, _ = x.shape
    n = w.shape[1]
    L = s // dil
    lt = min(L, T)
    n_in = T // lt
    xv = x.reshape(b, L, dil * D)
    if dil == 1:
        maps = [lambda i, j: (i, j, 0)]
    else:
        maps = [functools.partial(lambda i, j, k: (i, 0, n_in * j + k), k=k) for k in range(n_in)]
    return pl.pallas_call(
        functools.partial(_proj_kernel, n_in=n_in, lt=lt),
        out_shape=jax.ShapeDtypeStruct((b, s, n), BF16),
        grid=(b, s // T),
        in_specs=[pl.BlockSpec((None, lt, D), m) for m in maps]
        + [_const_spec((1, D)), _const_spec((D, n))],
        out_specs=pl.BlockSpec((None, T, n), lambda i, j: (i, j, 0)),
        scratch_shapes=[pltpu.VMEM((T, D), BF16)],
        compiler_params=pltpu.CompilerParams(
            dimension_semantics=("arbitrary", "arbitrary"),
            vmem_limit_bytes=VMEM_LIMIT),
        name=f"proj_d{dil}_n{n}",
    )(*([xv] * n_in), g.reshape(1, D), w)


def _attn_kernel(*refs, n_seq, nq, has_halo):
    if has_halo:
        q_ref, kv_ref, halo_ref, o_ref, lse_ref = refs
    else:
        q_ref, kv_ref, o_ref, lse_ref = refs
    first_tile = pl.program_id(1) == 0
    row = lax.broadcasted_iota(jnp.int32, (BLK, BLK), 0)
    col = lax.broadcasted_iota(jnp.int32, (BLK, BLK), 1)
    keep_cur = col <= row
    keep_prev = col >= row
    scale = 1.0 / math.sqrt(DH)
    dn = (((1,), (1,)), ((), ()))

    for i in range(n_seq):
        for n in range(nq):
            r0 = n * BLK
            lse_tile = jnp.zeros((BLK, BLK), F32)
            for h in range(H):
                c0 = h * DH
                q = q_ref[i, r0:r0 + BLK, c0:c0 + DH]
                kc = kv_ref[i, r0:r0 + BLK, c0:c0 + DH]
                vc = kv_ref[i, r0:r0 + BLK, D + c0:D + c0 + DH]
                s_cur = lax.dot_general(q, kc, dn, preferred_element_type=F32) * scale
                s_cur = jnp.where(keep_cur, s_cur, NEG)
                m = jnp.max(s_cur, axis=-1, keepdims=True)
                if n > 0:
                    kp = kv_ref[i, r0 - BLK:r0, c0:c0 + DH]
                    vp = kv_ref[i, r0 - BLK:r0, D + c0:D + c0 + DH]
                    keep = keep_prev
                elif has_halo:
                    kp = halo_ref[0, :, c0:c0 + DH]
                    vp = halo_ref[0, :, D + c0:D + c0 + DH]
                    keep = col >= row + jnp.where(first_tile, BLK, 0)
                else:
                    kp = None
                if kp is not None:
                    s_prev = lax.dot_general(q, kp, dn, preferred_element_type=F32) * scale
                    s_prev = jnp.where(keep, s_prev, NEG)
                    m = jnp.maximum(m, jnp.max(s_prev, axis=-1, keepdims=True))
                    p_prev = jnp.exp(s_prev - m)
                p_cur = jnp.exp(s_cur - m)
                den = jnp.sum(p_cur, axis=-1, keepdims=True)
                out = jnp.dot(p_cur.astype(BF16), vc, preferred_element_type=F32)
                if kp is not None:
                    den = den + jnp.sum(p_prev, axis=-1, keepdims=True)
                    out = out + jnp.dot(p_prev.astype(BF16), vp, preferred_element_type=F32)
                out = out / den
                o_ref[r0:r0 + BLK, i * D + c0:i * D + c0 + DH] = out.astype(BF16)
                lse_tile = jnp.where(col == h, m + jnp.log(den), lse_tile)
            lse_ref[r0:r0 + BLK, i * BLK:(i + 1) * BLK] = lse_tile


def _attn(q, kv, dil):
    b, s, _ = q.shape
    L = s // dil
    tq = min(L, T)
    nq = tq // BLK
    n_seq = T // tq
    has_halo = L > tq
    q4 = q.reshape(b, dil, L, D)
    kv4 = kv.reshape(b, dil, L, 2 * D)
    if dil == 1:
        seq_map = lambda i, j: (i, 0, j, 0)
        out_map = lambda i, j: (i, j, 0)
    else:
        seq_map = lambda i, j: (i, j, 0, 0)
        out_map = lambda i, j: (i, 0, j)
    in_specs = [pl.BlockSpec((None, n_seq, tq, D), seq_map),
                pl.BlockSpec((None, n_seq, tq, 2 * D), seq_map)]
    args = [q4, kv4]
    if has_halo:
        in_specs.append(pl.BlockSpec(
            (None, 1, BLK, 2 * D), lambda i, j: (i, 0, jnp.maximum(j * (tq // BLK) - 1, 0), 0)))
        args.append(kv4)
    out, lse = pl.pallas_call(
        functools.partial(_attn_kernel, n_seq=n_seq, nq=nq, has_halo=has_halo),
        out_shape=(jax.ShapeDtypeStruct((b, L, dil * D), BF16),
                   jax.ShapeDtypeStruct((b, L, dil * BLK), F32)),
        grid=(b, s // T),
        in_specs=in_specs,
        out_specs=(pl.BlockSpec((None, tq, n_seq * D), out_map),
                   pl.BlockSpec((None, tq, n_seq * BLK), out_map)),
        compiler_params=pltpu.CompilerParams(
            dimension_semantics=("arbitrary", "arbitrary"),
            vmem_limit_bytes=VMEM_LIMIT),
        name=f"attn_d{dil}",
    )(*args)
    return out.reshape(b, s, D), lse.reshape(b, s, BLK)


def _merge_kernel(x_ref, o0_ref, o1_ref, o2_ref, l0_ref, l1_ref, l2_ref, wo_ref, gpost_ref,
                  out_ref, mbuf_ref):
    o_refs = (o0_ref, o1_ref, o2_ref)
    l_refs = (l0_ref, l1_ref, l2_ref)
    for rb in range(T // RB):
        r0 = rb * RB
        ls = [l[r0:r0 + RB, :] for l in l_refs]
        mx = jnp.maximum(jnp.maximum(ls[0], ls[1]), ls[2])
        es = [jnp.exp(l - mx) for l in ls]
        tot = es[0] + es[1] + es[2]
        ws = [e / tot for e in es]
        for h in range(H):
            c0 = h * DH
            acc = ws[0][:, h:h + 1] * o_refs[0][r0:r0 + RB, c0:c0 + DH].astype(F32)
            acc = acc + ws[1][:, h:h + 1] * o_refs[1][r0:r0 + RB, c0:c0 + DH].astype(F32)
            acc = acc + ws[2][:, h:h + 1] * o_refs[2][r0:r0 + RB, c0:c0 + DH].astype(F32)
            mbuf_ref[r0:r0 + RB, c0:c0 + DH] = acc.astype(BF16)
    y = jnp.dot(mbuf_ref[...], wo_ref[...], preferred_element_type=F32)
    out_ref[...] = x_ref[...] + _rms(y, gpost_ref[...])


def _merge(x, outs, lses, w_o, gpost):
    b, s, _ = x.shape
    tile = lambda w: pl.BlockSpec((None, T, w), lambda i, j: (i, j, 0))
    return pl.pallas_call(
        _merge_kernel,
        out_shape=jax.ShapeDtypeStruct(x.shape, F32),
        grid=(b, s // T),
        in_specs=[tile(D)] + [tile(D)] * 3 + [tile(BLK)] * 3 + [_const_spec((D, D)), _const_spec((1, D))],
        out_specs=tile(D),
        scratch_shapes=[pltpu.VMEM((T, D), BF16)],
        compiler_params=pltpu.CompilerParams(
            dimension_semantics=("arbitrary", "arbitrary"),
            vmem_limit_bytes=VMEM_LIMIT),
        name="merge",
    )(x, *outs, *lses, w_o.astype(BF16), gpost.reshape(1, D))


def kernel(x, mix_pre_g, mix_post_g, ffn_pre_g, ffn_post_g, cm_w_in, cm_b_in, cm_dw, cm_dw_b,
           cm_ln_g, cm_ln_b, cm_w_out, cm_b_out, kv_norm_g, w_kv, w_q, w_o, ffn_w_in, ffn_dw,
           ffn_dw_b, ffn_w_out):
    n_a = cm_w_in.shape[0]
    depth = ffn_w_in.shape[0]
    qw = len(DILS) * D
    kvs = None
    for i in range(depth):
        if i < n_a:
            x = _conformer(x, mix_pre_g[i], cm_w_in[i], cm_b_in[i], cm_dw[i], cm_dw_b[i],
                           cm_ln_g[i], cm_ln_b[i], cm_w_out[i], cm_b_out[i], mix_post_g[i])
        else:
            j = i - n_a
            outs, lses = [], []
            for g, dil in enumerate(DILS):
                q = _proj(x, mix_pre_g[i], w_q[j][:, g * D:(g + 1) * D].astype(BF16), dil)
                o, l = _attn(q, kvs[g], dil)
                outs.append(o)
                lses.append(l)
            x = _merge(x, outs, lses, w_o[j], mix_post_g[i])
        x = _ffn(x, ffn_pre_g[i], ffn_w_in[i], ffn_dw[i], ffn_dw_b[i], ffn_w_out[i], ffn_post_g[i])
        if i == n_a - 1:
            kvs = []
            for g, dil in enumerate(DILS):
                w = jnp.concatenate([w_kv[:, g * D:(g + 1) * D],
                                     w_kv[:, qw + g * D:qw + (g + 1) * D]], axis=1).astype(BF16)
                kvs.append(_proj(x, kv_norm_g, w, dil))
    return x
```

```python
import functools
import math

import jax
import jax.numpy as jnp
from jax import lax
from jax.experimental import pallas as pl
from jax.experimental.pallas import tpu as pltpu

D = 1024
F = 2816
H = 8
DH = 128
DILS = (1, 4, 16)
BLK = 128
KW = 31
EPS = 1e-6
NEG = -1e30

T = 512
FC = 256
NC = F // FC
NS = 2 * FC // 128
HALO = 32
RB = 64
VMEM_LIMIT = 56 * 1024 * 1024

F32 = jnp.float32
BF16 = jnp.bfloat16


def _rms(x, g):
    return x * lax.rsqrt(jnp.mean(x * x, axis=-1, keepdims=True) + EPS) * g


def _rows_at(ref, slab, start, rows):
    return ref[pl.ds(slab, 1, stride=2), pl.ds(start, rows), :][0]


def _const_spec(shape):
    n = len(shape)
    return pl.BlockSpec(shape, lambda *_: (0,) * n, pipeline_mode=pl.Buffered(1))


def _ffn_kernel(x_ref, gpre_ref, win_ref, dwc_ref, wout_ref, gpost_ref, o_ref,
                h_ref, ubuf_ref, act_ref, acc_ref, tail_ref):
    s = pl.program_id(1)

    @pl.when(s == 0)
    def _():
        tail_ref[...] = jnp.zeros_like(tail_ref)

    h_ref[...] = _rms(x_ref[...], gpre_ref[...]).astype(BF16)
    acc_ref[...] = jnp.zeros_like(acc_ref)

    def up_half(c, slot, half):
        n0 = half * FC
        u = jnp.dot(h_ref[...], win_ref[c, :, n0:n0 + FC], preferred_element_type=F32)
        for k in range(FC // 128):
            sl = slot * NS + half * (FC // 128) + k
            ubuf_ref[sl, 0:8, :] = tail_ref[c, :, n0 + k * 128:n0 + (k + 1) * 128]
            ubuf_ref[sl, 8:8 + T, :] = u[:, k * 128:(k + 1) * 128]
        tail_ref[c, :, n0:n0 + FC] = u[T - 8:T, :]

    def gate_rows(c, slot, r0):
        def conv(sl, l0):
            w = dwc_ref[c, :, l0:l0 + 128]
            return (w[0:1] * _rows_at(ubuf_ref, sl, 6 + r0, RB)
                    + w[1:2] * _rows_at(ubuf_ref, sl, 7 + r0, RB)
                    + w[2:3] * ubuf_ref[sl, 8 + r0:8 + r0 + RB, :]) + w[3:4]

        for k in range(FC // 128):
            a = conv(slot * NS + k, k * 128)
            g = conv(slot * NS + FC // 128 + k, FC + k * 128)
            act_ref[slot, r0:r0 + RB, k * 128:(k + 1) * 128] = (
                (g / (1.0 + jnp.exp(-g))) * a).astype(BF16)

    def down(c, slot):
        acc_ref[...] += jnp.dot(act_ref[slot], wout_ref[c], preferred_element_type=F32)

    def step(c, slot, has_up=True, has_down=True):
        mxu = []
        if has_up:
            mxu += [functools.partial(up_half, c + 1, 1 - slot, 0),
                    functools.partial(up_half, c + 1, 1 - slot, 1)]
        if has_down:
            mxu += [functools.partial(down, c - 1, 1 - slot)]
        nblk = T // RB
        for rb in range(nblk):
            for f in mxu[rb * len(mxu) // nblk:(rb + 1) * len(mxu) // nblk]:
                f()
            gate_rows(c, slot, rb * RB)

    up_half(0, 0, 0)
    up_half(0, 0, 1)
    step(0, 0, has_down=False)

    def pair(i, carry):
        step(2 * i + 1, 1)
        step(2 * i + 2, 0)
        return carry

    lax.fori_loop(0, (NC - 3) // 2, pair, 0)
    step(NC - 2, 1)
    step(NC - 1, 0, has_up=False)
    down(NC - 1, 0)
    o_ref[...] = x_ref[...] + _rms(acc_ref[...], gpost_ref[...])


def _ffn(x, gpre, w_in, dw, dwb, w_out, gpost):
    b, s, _ = x.shape
    win_c = jnp.concatenate(
        [w_in[:, :F].reshape(D, NC, FC), w_in[:, F:].reshape(D, NC, FC)], axis=2
    ).transpose(1, 0, 2).astype(BF16)
    taps = jnp.concatenate([dw, dwb[None, :], jnp.zeros((4, 2 * F), F32)], axis=0)
    dwc = jnp.concatenate(
        [taps[:, :F].reshape(8, NC, FC), taps[:, F:].reshape(8, NC, FC)], axis=2
    ).transpose(1, 0, 2)
    wout_c = w_out.reshape(NC, FC, D).astype(BF16)
    return pl.pallas_call(
        _ffn_kernel,
        out_shape=jax.ShapeDtypeStruct(x.shape, F32),
        grid=(b, s // T),
        in_specs=[
            pl.BlockSpec((None, T, D), lambda i, j: (i, j, 0)),
            _const_spec((1, D)),
            _const_spec((NC, D, 2 * FC)),
            _const_spec((NC, 8, 2 * FC)),
            _const_spec((NC, FC, D)),
            _const_spec((1, D)),
        ],
        out_specs=pl.BlockSpec((None, T, D), lambda i, j: (i, j, 0)),
        scratch_shapes=[
            pltpu.VMEM((T, D), BF16),
            pltpu.VMEM((2 * NS, T + 8, 128), F32),
            pltpu.VMEM((2, T, FC), BF16),
            pltpu.VMEM((T, D), F32),
            pltpu.VMEM((NC, 8, 2 * FC), F32),
        ],
        compiler_params=pltpu.CompilerParams(
            dimension_semantics=("arbitrary", "arbitrary"),
            vmem_limit_bytes=VMEM_LIMIT),
        name="conv_ffn",
    )(x, gpre.reshape(1, D), win_c, dwc, wout_c, gpost.reshape(1, D))


def _conf_kernel(x_ref, gpre_ref, win_ref, bin_ref, dw_ref, dwb_ref, lng_ref, lnb_ref,
                 wout_ref, bout_ref, gpost_ref, o_ref, h_ref, gbuf_ref, cbuf_ref, sbuf_ref):
    s = pl.program_id(1)

    @pl.when(s == 0)
    def _():
        gbuf_ref[:, 0:HALO, :] = jnp.zeros((D // 128, HALO, 128), F32)

    @pl.when(s > 0)
    def _():
        gbuf_ref[:, 0:HALO, :] = gbuf_ref[:, T:T + HALO, :]

    h_ref[...] = _rms(x_ref[...], gpre_ref[...]).astype(BF16)
    CB = 256
    for cb in range(D // CB):
        c0 = cb * CB
        ua = jnp.dot(h_ref[...], win_ref[:, c0:c0 + CB], preferred_element_type=F32)
        ug = jnp.dot(h_ref[...], win_ref[:, D + c0:D + c0 + CB], preferred_element_type=F32)
        ua = ua + bin_ref[:, c0:c0 + CB]
        ug = ug + bin_ref[:, D + c0:D + c0 + CB]
        glu = ua * (1.0 / (1.0 + jnp.exp(-ug)))
        for k in range(CB // 128):
            gbuf_ref[cb * (CB // 128) + k, HALO:HALO + T, :] = glu[:, k * 128:(k + 1) * 128]

    off = HALO - (KW - 1)
    for lb in range(D // 128):
        l0 = lb * 128

        def conv_rows(rb, carry, lb=lb, l0=l0):
            r0 = pl.multiple_of(rb * RB, RB)
            acc = jnp.broadcast_to(dwb_ref[:, l0:l0 + 128], (RB, 128))
            for j in range(KW):
                acc = acc + dw_ref[j:j + 1, l0:l0 + 128] * _rows_at(gbuf_ref, lb, r0 + off + j, RB)
            cbuf_ref[pl.ds(r0, RB), l0:l0 + 128] = acc
            return carry

        lax.fori_loop(0, T // RB, conv_rows, 0)

    def ln_rows(rb, carry):
        r0 = pl.multiple_of(rb * RB, RB)
        c = cbuf_ref[pl.ds(r0, RB), :]
        mu = jnp.mean(c, axis=-1, keepdims=True)
        cc = c - mu
        var = jnp.mean(cc * cc, axis=-1, keepdims=True)
        y = cc * lax.rsqrt(var + EPS) * lng_ref[...] + lnb_ref[...]
        sbuf_ref[pl.ds(r0, RB), :] = (y * (1.0 / (1.0 + jnp.exp(-y)))).astype(BF16)
        return carry

    lax.fori_loop(0, T // RB, ln_rows, 0)

    y = jnp.dot(sbuf_ref[...], wout_ref[...], preferred_element_type=F32) + bout_ref[...]
    o_ref[...] = x_ref[...] + _rms(y, gpost_ref[...])


def _conformer(x, gpre, w_in, b_in, dw, dwb, lng, lnb, w_out, b_out, gpost):
    b, s, _ = x.shape
    dwp = jnp.concatenate([dw, jnp.zeros((32 - KW, D), F32)], axis=0)
    return pl.pallas_call(
        _conf_kernel,
        out_shape=jax.ShapeDtypeStruct(x.shape, F32),
        grid=(b, s // T),
        in_specs=[
            pl.BlockSpec((None, T, D), lambda i, j: (i, j, 0)),
            _const_spec((1, D)),
            _const_spec((D, 2 * D)),
            _const_spec((1, 2 * D)),
            _const_spec((32, D)),
            _const_spec((1, D)),
            _const_spec((1, D)),
            _const_spec((1, D)),
            _const_spec((D, D)),
            _const_spec((1, D)),
            _const_spec((1, D)),
        ],
        out_specs=pl.BlockSpec((None, T, D), lambda i, j: (i, j, 0)),
        scratch_shapes=[
            pltpu.VMEM((T, D), BF16),
            pltpu.VMEM((D // 128, T + HALO, 128), F32),
            pltpu.VMEM((T, D), F32),
            pltpu.VMEM((T, D), BF16),
        ],
        compiler_params=pltpu.CompilerParams(
            dimension_semantics=("arbitrary", "arbitrary"),
            vmem_limit_bytes=VMEM_LIMIT),
        name="conformer",
    )(x, gpre.reshape(1, D), w_in.astype(BF16), b_in.reshape(1, 2 * D), dwp, dwb.reshape(1, D),
      lng.reshape(1, D), lnb.reshape(1, D), w_out.astype(BF16), b_out.reshape(1, D),
      gpost.reshape(1, D))


NB = 512


def _proj_kernel(x_ref, g_ref, w_ref, o_ref, h_ref, slab_ref, *, dil):
    h_ref[...] = _rms(x_ref[...], g_ref[...]).astype(BF16)
    n = w_ref.shape[1]
    lt = T // dil
    for nb in range(n // NB):
        res = jnp.dot(h_ref[...], w_ref[:, nb * NB:(nb + 1) * NB], preferred_element_type=F32)
        if dil == 1:
            o_ref[0, :, nb * NB:(nb + 1) * NB] = res.astype(BF16)
            continue
        for k in range(NB // 128):
            slab_ref[k] = res[:, k * 128:(k + 1) * 128]
        for k in range(NB // 128):
            c0 = nb * NB + k * 128
            for r in range(dil):
                o_ref[r, :, c0:c0 + 128] = slab_ref[k, pl.ds(r, lt, stride=dil), :].astype(BF16)


def _proj(x, g, w, dil):
    b, s, _ = x.shape
    n = w.shape[1]
    L = s // dil
    lt = T // dil
    return pl.pallas_call(
        functools.partial(_proj_kernel, dil=dil),
        out_shape=jax.ShapeDtypeStruct((b, dil, L, n), BF16),
        grid=(b, s // T),
        in_specs=[pl.BlockSpec((None, T, D), lambda i, j: (i, j, 0)),
                  _const_spec((1, D)), _const_spec((D, n))],
        out_specs=pl.BlockSpec((None, dil, lt, n), lambda i, j: (i, 0, j, 0)),
        scratch_shapes=[pltpu.VMEM((T, D), BF16), pltpu.VMEM((NB // 128, T, 128), F32)],
        compiler_params=pltpu.CompilerParams(
            dimension_semantics=("arbitrary", "arbitrary"),
            vmem_limit_bytes=VMEM_LIMIT),
        name=f"proj_d{dil}_n{n}",
    )(x, g.reshape(1, D), w)


def _attn_kernel(*refs, dil, n_seq, nq, has_halo):
    if has_halo:
        q_ref, kv_ref, halo_ref, o_ref, lse_ref = refs
    else:
        q_ref, kv_ref, o_ref, lse_ref = refs
    step = pl.program_id(1)
    first_tile = step == 0

    def token_rows(i, n):
        if dil == 1:
            return pl.ds(n * BLK, BLK)
        if has_halo:
            return pl.ds(i, BLK, stride=dil)
        return pl.ds(step * n_seq + i, BLK, stride=dil)

    row = lax.broadcasted_iota(jnp.int32, (BLK, BLK), 0)
    col = lax.broadcasted_iota(jnp.int32, (BLK, BLK), 1)
    keep_cur = col <= row
    keep_prev = col >= row
    scale = 1.0 / math.sqrt(DH)
    dn = (((1,), (1,)), ((), ()))

    for i in range(n_seq):
        for n in range(nq):
            r0 = n * BLK
            lse_tile = jnp.zeros((BLK, BLK), F32)
            for h in range(H):
                c0 = h * DH
                q = q_ref[i, r0:r0 + BLK, c0:c0 + DH]
                kc = kv_ref[i, r0:r0 + BLK, c0:c0 + DH]
                vc = kv_ref[i, r0:r0 + BLK, D + c0:D + c0 + DH]
                s_cur = lax.dot_general(q, kc, dn, preferred_element_type=F32) * scale
                s_cur = jnp.where(keep_cur, s_cur, NEG)
                m = jnp.max(s_cur, axis=-1, keepdims=True)
                if n > 0:
                    kp = kv_ref[i, r0 - BLK:r0, c0:c0 + DH]
                    vp = kv_ref[i, r0 - BLK:r0, D + c0:D + c0 + DH]
                    keep = keep_prev
                elif has_halo:
                    kp = halo_ref[i, :, c0:c0 + DH]
                    vp = halo_ref[i, :, D + c0:D + c0 + DH]
                    keep = col >= row + jnp.where(first_tile, BLK, 0)
                else:
                    kp = None
                if kp is not None:
                    s_prev = lax.dot_general(q, kp, dn, preferred_element_type=F32) * scale
                    s_prev = jnp.where(keep, s_prev, NEG)
                    m = jnp.maximum(m, jnp.max(s_prev, axis=-1, keepdims=True))
                    p_prev = jnp.exp(s_prev - m)
                p_cur = jnp.exp(s_cur - m)
                den = jnp.sum(p_cur, axis=-1, keepdims=True)
                out = jnp.dot(p_cur.astype(BF16), vc, preferred_element_type=F32)
                if kp is not None:
                    den = den + jnp.sum(p_prev, axis=-1, keepdims=True)
                    out = out + jnp.dot(p_prev.astype(BF16), vp, preferred_element_type=F32)
                out = out / den
                o_ref[h, token_rows(i, n), :] = out
                lse_tile = jnp.where(col == h, m + jnp.log(den), lse_tile)
            lse_ref[token_rows(i, n), :] = lse_tile


def _attn(q, kv, dil):
    b, _, L, _ = q.shape
    s = L * dil
    n_seq = min(dil, T // BLK)
    nq = T // (n_seq * BLK)
    has_halo = L > nq * BLK
    if has_halo:
        seq_map = lambda i, j: (i, 0, j, 0)
        out_rows = T
        out_map = lambda i, j: (i, 0, j, 0)
        lse_map = lambda i, j: (i, j, 0)
    else:
        seq_map = lambda i, j: (i, j, 0, 0)
        out_rows = s
        out_map = lambda i, j: (i, 0, 0, 0)
        lse_map = lambda i, j: (i, 0, 0)
    in_specs = [pl.BlockSpec((None, n_seq, nq * BLK, D), seq_map),
                pl.BlockSpec((None, n_seq, nq * BLK, 2 * D), seq_map)]
    args = [q, kv]
    if has_halo:
        in_specs.append(pl.BlockSpec(
            (None, n_seq, BLK, 2 * D), lambda i, j: (i, 0, jnp.maximum(j * nq - 1, 0), 0)))
        args.append(kv)
    return pl.pallas_call(
        functools.partial(_attn_kernel, dil=dil, n_seq=n_seq, nq=nq, has_halo=has_halo),
        out_shape=(jax.ShapeDtypeStruct((b, H, s, DH), F32),
                   jax.ShapeDtypeStruct((b, s, BLK), F32)),
        grid=(b, s // T),
        in_specs=in_specs,
        out_specs=(pl.BlockSpec((None, H, out_rows, DH), out_map),
                   pl.BlockSpec((None, out_rows, BLK), lse_map)),
        compiler_params=pltpu.CompilerParams(
            dimension_semantics=("arbitrary", "arbitrary"),
            vmem_limit_bytes=VMEM_LIMIT),
        name=f"attn_d{dil}",
    )(*args)


def _merge_kernel(x_ref, o0_ref, o1_ref, o2_ref, l0_ref, l1_ref, l2_ref, wo_ref, gpost_ref,
                  out_ref, mbuf_ref):
    o_refs = (o0_ref, o1_ref, o2_ref)
    l_refs = (l0_ref, l1_ref, l2_ref)
    for rb in range(T // RB):
        r0 = rb * RB
        ls = [l[r0:r0 + RB, :] for l in l_refs]
        mx = jnp.maximum(jnp.maximum(ls[0], ls[1]), ls[2])
        es = [jnp.exp(l - mx) for l in ls]
        tot = es[0] + es[1] + es[2]
        ws = [e / tot for e in es]
        for h in range(H):
            c0 = h * DH
            acc = ws[0][:, h:h + 1] * o_refs[0][h, r0:r0 + RB, :]
            acc = acc + ws[1][:, h:h + 1] * o_refs[1][h, r0:r0 + RB, :]
            acc = acc + ws[2][:, h:h + 1] * o_refs[2][h, r0:r0 + RB, :]
            mbuf_ref[r0:r0 + RB, c0:c0 + DH] = acc.astype(BF16)
    y = jnp.dot(mbuf_ref[...], wo_ref[...], preferred_element_type=F32)
    out_ref[...] = x_ref[...] + _rms(y, gpost_ref[...])


def _merge(x, outs, lses, w_o, gpost):
    b, s, _ = x.shape
    tile = lambda w: pl.BlockSpec((None, T, w), lambda i, j: (i, j, 0))
    heads = pl.BlockSpec((None, H, T, DH), lambda i, j: (i, 0, j, 0))
    return pl.pallas_call(
        _merge_kernel,
        out_shape=jax.ShapeDtypeStruct(x.shape, F32),
        grid=(b, s // T),
        in_specs=[tile(D)] + [heads] * 3 + [tile(BLK)] * 3 + [_const_spec((D, D)), _const_spec((1, D))],
        out_specs=tile(D),
        scratch_shapes=[pltpu.VMEM((T, D), BF16)],
        compiler_params=pltpu.CompilerParams(
            dimension_semantics=("arbitrary", "arbitrary"),
            vmem_limit_bytes=VMEM_LIMIT),
        name="merge",
    )(x, *outs, *lses, w_o.astype(BF16), gpost.reshape(1, D))


def kernel(x, mix_pre_g, mix_post_g, ffn_pre_g, ffn_post_g, cm_w_in, cm_b_in, cm_dw, cm_dw_b,
           cm_ln_g, cm_ln_b, cm_w_out, cm_b_out, kv_norm_g, w_kv, w_q, w_o, ffn_w_in, ffn_dw,
           ffn_dw_b, ffn_w_out):
    n_a = cm_w_in.shape[0]
    depth = ffn_w_in.shape[0]
    qw = len(DILS) * D
    kvs = None
    for i in range(depth):
        if i < n_a:
            x = _conformer(x, mix_pre_g[i], cm_w_in[i], cm_b_in[i], cm_dw[i], cm_dw_b[i],
                           cm_ln_g[i], cm_ln_b[i], cm_w_out[i], cm_b_out[i], mix_post_g[i])
        else:
            j = i - n_a
            outs, lses = [], []
            for g, dil in enumerate(DILS):
                q = _proj(x, mix_pre_g[i], w_q[j][:, g * D:(g + 1) * D].astype(BF16), dil)
                o, l = _attn(q, kvs[g], dil)
                outs.append(o)
                lses.append(l)
            x = _merge(x, outs, lses, w_o[j], mix_post_g[i])
        x = _ffn(x, ffn_pre_g[i], ffn_w_in[i], ffn_dw[i], ffn_dw_b[i], ffn_w_out[i], ffn_post_g[i])
        if i == n_a - 1:
            kvs = []
            for g, dil in enumerate(DILS):
                w = jnp.concatenate([w_kv[:, g * D:(g + 1) * D],
                                     w_kv[:, qw + g * D:qw + (g + 1) * D]], axis=1).astype(BF16)
                kvs.append(_proj(x, kv_norm_g, w, dil))
    return x
```

```python
import functools
import math

import jax
import jax.numpy as jnp
from jax import lax
from jax.experimental import pallas as pl
from jax.experimental.pallas import tpu as pltpu

D = 1024
F = 2816
H = 8
DH = 128
DILS = (1, 4, 16)
BLK = 128
KW = 31
EPS = 1e-6
NEG = -1e30

T = 512
FC = 256
NC = F // FC
NS = 2 * FC // 128
HALO = 32
RB = 64
GRB = 64
VMEM_LIMIT = 56 * 1024 * 1024

F32 = jnp.float32
BF16 = jnp.bfloat16


def _rms(x, g):
    return x * lax.rsqrt(jnp.mean(x * x, axis=-1, keepdims=True) + EPS) * g


def _rows_at(ref, slab, start, rows):
    return ref[pl.ds(slab, 1, stride=2), pl.ds(start, rows), :][0]


def _const_spec(shape):
    n = len(shape)
    return pl.BlockSpec(shape, lambda *_: (0,) * n, pipeline_mode=pl.Buffered(1))


def _ffn_kernel(x_ref, gpre_ref, win_ref, dwc_ref, wout_ref, gpost_ref, o_ref,
                h_ref, ubuf_ref, act_ref, y_ref, tail_ref):
    s = pl.program_id(1)

    @pl.when(s == 0)
    def _():
        tail_ref[...] = jnp.zeros_like(tail_ref)

    for rb in range(T // RB):
        rows = slice(rb * RB, (rb + 1) * RB)
        h_ref[rows, :] = _rms(x_ref[rows, :], gpre_ref[...]).astype(BF16)

    def up_half(c, slot, half):
        n0 = half * FC
        u = jnp.dot(h_ref[...], win_ref[c, :, n0:n0 + FC], preferred_element_type=F32)
        for k in range(FC // 128):
            sl = slot * NS + half * (FC // 128) + k
            ubuf_ref[sl, 0:8, :] = tail_ref[c, :, n0 + k * 128:n0 + (k + 1) * 128]
            ubuf_ref[sl, 8:8 + T, :] = u[:, k * 128:(k + 1) * 128]
        tail_ref[c, :, n0:n0 + FC] = u[T - 8:T, :]

    def gate_rows(c, slot, r0):
        def conv(sl, l0):
            w = dwc_ref[c, :, l0:l0 + 128]
            return (w[0:1] * _rows_at(ubuf_ref, sl, 6 + r0, GRB)
                    + w[1:2] * _rows_at(ubuf_ref, sl, 7 + r0, GRB)
                    + w[2:3] * ubuf_ref[sl, 8 + r0:8 + r0 + GRB, :]) + w[3:4]

        for k in range(FC // 128):
            a = conv(slot * NS + k, k * 128)
            g = conv(slot * NS + FC // 128 + k, FC + k * 128)
            act_ref[r0:r0 + GRB, c * FC + k * 128:c * FC + (k + 1) * 128] = (
                (g / (1.0 + jnp.exp(-g))) * a).astype(BF16)

    up_half(0, 0, 0)
    up_half(0, 0, 1)
    for c in range(NC):
        slot = c % 2
        if c + 1 < NC:
            up_half(c + 1, 1 - slot, 0)
            up_half(c + 1, 1 - slot, 1)
        for rb in range(T // GRB):
            gate_rows(c, slot, rb * GRB)
    y_ref[...] = jnp.dot(act_ref[...], wout_ref[...], preferred_element_type=F32)
    for rb in range(T // RB):
        rows = slice(rb * RB, (rb + 1) * RB)
        o_ref[rows, :] = x_ref[rows, :] + _rms(y_ref[rows, :], gpost_ref[...])


def _ffn(x, gpre, w_in, dw, dwb, w_out, gpost):
    b, s, _ = x.shape
    win_c = jnp.concatenate(
        [w_in[:, :F].reshape(D, NC, FC), w_in[:, F:].reshape(D, NC, FC)], axis=2
    ).transpose(1, 0, 2).astype(BF16)
    taps = jnp.concatenate([dw, dwb[None, :], jnp.zeros((4, 2 * F), F32)], axis=0)
    dwc = jnp.concatenate(
        [taps[:, :F].reshape(8, NC, FC), taps[:, F:].reshape(8, NC, FC)], axis=2
    ).transpose(1, 0, 2)
    wout_c = w_out.astype(BF16)
    return pl.pallas_call(
        _ffn_kernel,
        out_shape=jax.ShapeDtypeStruct(x.shape, F32),
        grid=(b, s // T),
        in_specs=[
            pl.BlockSpec((None, T, D), lambda i, j: (i, j, 0)),
            _const_spec((1, D)),
            _const_spec((NC, D, 2 * FC)),
            _const_spec((NC, 8, 2 * FC)),
            _const_spec((F, D)),
            _const_spec((1, D)),
        ],
        out_specs=pl.BlockSpec((None, T, D), lambda i, j: (i, j, 0)),
        scratch_shapes=[
            pltpu.VMEM((T, D), BF16),
            pltpu.VMEM((2 * NS, T + 8, 128), F32),
            pltpu.VMEM((T, F), BF16),
            pltpu.VMEM((T, D), F32),
            pltpu.VMEM((NC, 8, 2 * FC), F32),
        ],
        compiler_params=pltpu.CompilerParams(
            dimension_semantics=("arbitrary", "arbitrary"),
            vmem_limit_bytes=VMEM_LIMIT),
        name="conv_ffn",
    )(x, gpre.reshape(1, D), win_c, dwc, wout_c, gpost.reshape(1, D))


def _conf_kernel(x_ref, gpre_ref, win_ref, bin_ref, dw_ref, dwb_ref, lng_ref, lnb_ref,
                 wout_ref, bout_ref, gpost_ref, o_ref, h_ref, gbuf_ref, cbuf_ref, sbuf_ref):
    s = pl.program_id(1)

    @pl.when(s == 0)
    def _():
        gbuf_ref[:, 0:HALO, :] = jnp.zeros((D // 128, HALO, 128), F32)

    @pl.when(s > 0)
    def _():
        gbuf_ref[:, 0:HALO, :] = gbuf_ref[:, T:T + HALO, :]

    h_ref[...] = _rms(x_ref[...], gpre_ref[...]).astype(BF16)
    CB = 256
    for cb in range(D // CB):
        c0 = cb * CB
        ua = jnp.dot(h_ref[...], win_ref[:, c0:c0 + CB], preferred_element_type=F32)
        ug = jnp.dot(h_ref[...], win_ref[:, D + c0:D + c0 + CB], preferred_element_type=F32)
        ua = ua + bin_ref[:, c0:c0 + CB]
        ug = ug + bin_ref[:, D + c0:D + c0 + CB]
        glu = ua * (1.0 / (1.0 + jnp.exp(-ug)))
        for k in range(CB // 128):
            gbuf_ref[cb * (CB // 128) + k, HALO:HALO + T, :] = glu[:, k * 128:(k + 1) * 128]

    off = HALO - (KW - 1)
    for lb in range(D // 128):
        l0 = lb * 128

        def conv_rows(rb, carry, lb=lb, l0=l0):
            r0 = pl.multiple_of(rb * RB, RB)
            acc = jnp.broadcast_to(dwb_ref[:, l0:l0 + 128], (RB, 128))
            for j in range(KW):
                acc = acc + dw_ref[j:j + 1, l0:l0 + 128] * _rows_at(gbuf_ref, lb, r0 + off + j, RB)
            cbuf_ref[pl.ds(r0, RB), l0:l0 + 128] = acc
            return carry

        lax.fori_loop(0, T // RB, conv_rows, 0)

    def ln_rows(rb, carry):
        r0 = pl.multiple_of(rb * RB, RB)
        c = cbuf_ref[pl.ds(r0, RB), :]
        mu = jnp.mean(c, axis=-1, keepdims=True)
        cc = c - mu
        var = jnp.mean(cc * cc, axis=-1, keepdims=True)
        y = cc * lax.rsqrt(var + EPS) * lng_ref[...] + lnb_ref[...]
        sbuf_ref[pl.ds(r0, RB), :] = (y * (1.0 / (1.0 + jnp.exp(-y)))).astype(BF16)
        return carry

    lax.fori_loop(0, T // RB, ln_rows, 0, unroll=2)

    y = jnp.dot(sbuf_ref[...], wout_ref[...], preferred_element_type=F32) + bout_ref[...]
    o_ref[...] = x_ref[...] + _rms(y, gpost_ref[...])


def _conformer(x, gpre, w_in, b_in, dw, dwb, lng, lnb, w_out, b_out, gpost):
    b, s, _ = x.shape
    dwp = jnp.concatenate([dw, jnp.zeros((32 - KW, D), F32)], axis=0)
    return pl.pallas_call(
        _conf_kernel,
        out_shape=jax.ShapeDtypeStruct(x.shape, F32),
        grid=(b, s // T),
        in_specs=[
            pl.BlockSpec((None, T, D), lambda i, j: (i, j, 0)),
            _const_spec((1, D)),
            _const_spec((D, 2 * D)),
            _const_spec((1, 2 * D)),
            _const_spec((32, D)),
            _const_spec((1, D)),
            _const_spec((1, D)),
            _const_spec((1, D)),
            _const_spec((D, D)),
            _const_spec((1, D)),
            _const_spec((1, D)),
        ],
        out_specs=pl.BlockSpec((None, T, D), lambda i, j: (i, j, 0)),
        scratch_shapes=[
            pltpu.VMEM((T, D), BF16),
            pltpu.VMEM((D // 128, T + HALO, 128), F32),
            pltpu.VMEM((T, D), F32),
            pltpu.VMEM((T, D), BF16),
        ],
        compiler_params=pltpu.CompilerParams(
            dimension_semantics=("arbitrary", "arbitrary"),
            vmem_limit_bytes=VMEM_LIMIT),
        name="conformer",
    )(x, gpre.reshape(1, D), w_in.astype(BF16), b_in.reshape(1, 2 * D), dwp, dwb.reshape(1, D),
      lng.reshape(1, D), lnb.reshape(1, D), w_out.astype(BF16), b_out.reshape(1, D),
      gpost.reshape(1, D))


NB = 512
HG = 4


def _proj_kernel(x_ref, g_ref, *refs):
    nbr = len(DILS)
    w_refs, o_refs = refs[:nbr], refs[nbr:2 * nbr]
    h_ref, slab_ref, tmp_ref = refs[2 * nbr:]
    h_ref[...] = _rms(x_ref[...], g_ref[...]).astype(BF16)
    for w_ref, o_ref, dil in zip(w_refs, o_refs, DILS):
        for nb in range(w_ref.shape[1] // NB):
            res = jnp.dot(h_ref[...], w_ref[:, nb * NB:(nb + 1) * NB], preferred_element_type=F32)
            if dil == 1:
                o_ref[0, :, nb * NB:(nb + 1) * NB] = res.astype(BF16)
                continue
            for k in range(NB // 128):
                slab_ref[k] = res[:, k * 128:(k + 1) * 128]
            for k in range(NB // 128):
                c0 = nb * NB + k * 128
                if dil == 4:
                    for r in range(4):
                        o_ref[r, :, c0:c0 + 128] = (
                            slab_ref[k, pl.ds(r, T // 4, stride=4), :].astype(BF16))
                else:
                    for q in range(4):
                        tmp_ref[q] = slab_ref[k, pl.ds(q, T // 4, stride=4), :]
                    for q in range(4):
                        for p in range(4):
                            o_ref[q + 4 * p, :, c0:c0 + 128] = (
                                tmp_ref[q, pl.ds(p, T // 16, stride=4), :].astype(BF16))


def _proj(x, g, ws):
    assert DILS == (1, 4, 16)
    b, s, _ = x.shape
    n = ws[0].shape[1]
    return pl.pallas_call(
        _proj_kernel,
        out_shape=[jax.ShapeDtypeStruct((b, dil, s // dil, n), BF16) for dil in DILS],
        grid=(b, s // T),
        in_specs=[pl.BlockSpec((None, T, D), lambda i, j: (i, j, 0)), _const_spec((1, D))]
        + [_const_spec((D, n))] * len(DILS),
        out_specs=[pl.BlockSpec((None, dil, T // dil, n), lambda i, j: (i, 0, j, 0))
                   for dil in DILS],
        scratch_shapes=[pltpu.VMEM((T, D), BF16), pltpu.VMEM((NB // 128, T, 128), F32),
                        pltpu.VMEM((4, T // 4, 128), F32)],
        compiler_params=pltpu.CompilerParams(
            dimension_semantics=("arbitrary", "arbitrary"),
            vmem_limit_bytes=VMEM_LIMIT),
        name=f"proj_n{n}",
    )(x, g.reshape(1, D), *ws)


def _attn_kernel(*refs, dil, n_seq, nq, has_halo):
    if has_halo:
        q_ref, kv_ref, halo_ref, o_ref, lse_ref = refs
    else:
        q_ref, kv_ref, o_ref, lse_ref = refs
    step = pl.program_id(1)
    first_tile = step == 0

    def token_rows(i, n):
        if dil == 1:
            return pl.ds(n * BLK, BLK)
        if has_halo:
            return pl.ds(i, BLK, stride=dil)
        return pl.ds(step * n_seq + i, BLK, stride=dil)

    row = lax.broadcasted_iota(jnp.int32, (BLK, BLK), 0)
    col = lax.broadcasted_iota(jnp.int32, (BLK, BLK), 1)
    bias_cur = jnp.where(col <= row, 0.0, NEG)
    bias_prev = jnp.where(col >= row, 0.0, NEG)
    bias_halo = jnp.where(col >= row + jnp.where(first_tile, BLK, 0), 0.0, NEG)
    scale = 1.0 / math.sqrt(DH)
    exp2_scale = scale * math.log2(math.e)
    dn = (((1,), (1,)), ((), ()))

    def scores(i, n, h):
        r0, c0 = n * BLK, h * DH
        q = q_ref[i, r0:r0 + BLK, c0:c0 + DH]
        if n > 0:
            k = kv_ref[i, r0 - BLK:r0 + BLK, c0:c0 + DH]
            s = lax.dot_general(q, k, dn, preferred_element_type=F32)
            return s + jnp.concatenate([bias_prev, bias_cur], axis=1)
        s_cur = lax.dot_general(q, kv_ref[i, r0:r0 + BLK, c0:c0 + DH], dn,
                                preferred_element_type=F32) + bias_cur
        if not has_halo:
            return s_cur
        s_prev = lax.dot_general(q, halo_ref[i, :, c0:c0 + DH], dn,
                                 preferred_element_type=F32) + bias_halo
        return jnp.concatenate([s_prev, s_cur], axis=1)

    def weighted_values(i, n, h, p):
        r0, c0 = n * BLK, D + h * DH
        if n > 0:
            return jnp.dot(p, kv_ref[i, r0 - BLK:r0 + BLK, c0:c0 + DH], preferred_element_type=F32)
        if not has_halo:
            return jnp.dot(p, kv_ref[i, r0:r0 + BLK, c0:c0 + DH], preferred_element_type=F32)
        return (jnp.dot(p[:, :BLK], halo_ref[i, :, c0:c0 + DH], preferred_element_type=F32)
                + jnp.dot(p[:, BLK:], kv_ref[i, r0:r0 + BLK, c0:c0 + DH],
                          preferred_element_type=F32))

    for i in range(n_seq):
        for n in range(nq):
            lse_tile = jnp.zeros((BLK, BLK), F32)
            for h0 in range(0, H, HG):
                heads = range(h0, h0 + HG)
                s = [scores(i, n, h) for h in heads]
                m = [jnp.max(x, axis=-1, keepdims=True) for x in s]
                p = [jnp.exp2((x - mx) * exp2_scale) for x, mx in zip(s, m)]
                den = [jnp.sum(x, axis=-1, keepdims=True) for x in p]
                out = [weighted_values(i, n, h, x.astype(BF16)) for h, x in zip(heads, p)]
                for h, o, dd, mx in zip(heads, out, den, m):
                    o_ref[h, token_rows(i, n), :] = o / dd
                    lse_tile = jnp.where(col == h, mx * scale + jnp.log(dd), lse_tile)
            lse_ref[token_rows(i, n), :] = lse_tile


def _attn(q, kv, dil):
    b, _, L, _ = q.shape
    s = L * dil
    n_seq = min(dil, T // BLK)
    nq = T // (n_seq * BLK)
    has_halo = L > nq * BLK
    if has_halo:
        seq_map = lambda i, j: (i, 0, j, 0)
        out_rows = T
        out_map = lambda i, j: (i, 0, j, 0)
        lse_map = lambda i, j: (i, j, 0)
    else:
        seq_map = lambda i, j: (i, j, 0, 0)
        out_rows = s
        out_map = lambda i, j: (i, 0, 0, 0)
        lse_map = lambda i, j: (i, 0, 0)
    in_specs = [pl.BlockSpec((None, n_seq, nq * BLK, D), seq_map),
                pl.BlockSpec((None, n_seq, nq * BLK, 2 * D), seq_map)]
    args = [q, kv]
    if has_halo:
        in_specs.append(pl.BlockSpec(
            (None, n_seq, BLK, 2 * D), lambda i, j: (i, 0, jnp.maximum(j * nq - 1, 0), 0)))
        args.append(kv)
    return pl.pallas_call(
        functools.partial(_attn_kernel, dil=dil, n_seq=n_seq, nq=nq, has_halo=has_halo),
        out_shape=(jax.ShapeDtypeStruct((b, H, s, DH), F32),
                   jax.ShapeDtypeStruct((b, s, BLK), F32)),
        grid=(b, s // T),
        in_specs=in_specs,
        out_specs=(pl.BlockSpec((None, H, out_rows, DH), out_map),
                   pl.BlockSpec((None, out_rows, BLK), lse_map)),
        compiler_params=pltpu.CompilerParams(
            dimension_semantics=("arbitrary", "arbitrary"),
            vmem_limit_bytes=VMEM_LIMIT),
        name=f"attn_d{dil}",
    )(*args)


def _merge_kernel(x_ref, o0_ref, o1_ref, o2_ref, l0_ref, l1_ref, l2_ref, wo_ref, gpost_ref,
                  out_ref, mbuf_ref):
    o_refs = (o0_ref, o1_ref, o2_ref)
    l_refs = (l0_ref, l1_ref, l2_ref)
    for rb in range(T // RB):
        r0 = rb * RB
        ls = [l[r0:r0 + RB, :] for l in l_refs]
        mx = jnp.maximum(jnp.maximum(ls[0], ls[1]), ls[2])
        es = [jnp.exp(l - mx) for l in ls]
        tot = es[0] + es[1] + es[2]
        ws = [e / tot for e in es]
        for h in range(H):
            c0 = h * DH
            acc = ws[0][:, h:h + 1] * o_refs[0][h, r0:r0 + RB, :]
            acc = acc + ws[1][:, h:h + 1] * o_refs[1][h, r0:r0 + RB, :]
            acc = acc + ws[2][:, h:h + 1] * o_refs[2][h, r0:r0 + RB, :]
            mbuf_ref[r0:r0 + RB, c0:c0 + DH] = acc.astype(BF16)
    y = jnp.dot(mbuf_ref[...], wo_ref[...], preferred_element_type=F32)
    out_ref[...] = x_ref[...] + _rms(y, gpost_ref[...])


def _merge(x, outs, lses, w_o, gpost):
    b, s, _ = x.shape
    tile = lambda w: pl.BlockSpec((None, T, w), lambda i, j: (i, j, 0))
    heads = pl.BlockSpec((None, H, T, DH), lambda i, j: (i, 0, j, 0))
    return pl.pallas_call(
        _merge_kernel,
        out_shape=jax.ShapeDtypeStruct(x.shape, F32),
        grid=(b, s // T),
        in_specs=[tile(D)] + [heads] * 3 + [tile(BLK)] * 3 + [_const_spec((D, D)), _const_spec((1, D))],
        out_specs=tile(D),
        scratch_shapes=[pltpu.VMEM((T, D), BF16)],
        compiler_params=pltpu.CompilerParams(
            dimension_semantics=("arbitrary", "arbitrary"),
            vmem_limit_bytes=VMEM_LIMIT),
        name="merge",
    )(x, *outs, *lses, w_o.astype(BF16), gpost.reshape(1, D))


def kernel(x, mix_pre_g, mix_post_g, ffn_pre_g, ffn_post_g, cm_w_in, cm_b_in, cm_dw, cm_dw_b,
           cm_ln_g, cm_ln_b, cm_w_out, cm_b_out, kv_norm_g, w_kv, w_q, w_o, ffn_w_in, ffn_dw,
           ffn_dw_b, ffn_w_out):
    n_a = cm_w_in.shape[0]
    depth = ffn_w_in.shape[0]
    qw = len(DILS) * D
    kvs = None
    for i in range(depth):
        if i < n_a:
            x = _conformer(x, mix_pre_g[i], cm_w_in[i], cm_b_in[i], cm_dw[i], cm_dw_b[i],
                           cm_ln_g[i], cm_ln_b[i], cm_w_out[i], cm_b_out[i], mix_post_g[i])
        else:
            j = i - n_a
            qs = _proj(x, mix_pre_g[i],
                       [w_q[j][:, g * D:(g + 1) * D].astype(BF16) for g in range(len(DILS))])
            outs, lses = [], []
            for g, dil in enumerate(DILS):
                o, l = _attn(qs[g], kvs[g], dil)
                outs.append(o)
                lses.append(l)
            x = _merge(x, outs, lses, w_o[j], mix_post_g[i])
        x = _ffn(x, ffn_pre_g[i], ffn_w_in[i], ffn_dw[i], ffn_dw_b[i], ffn_w_out[i], ffn_post_g[i])
        if i == n_a - 1:
            kvs = _proj(x, kv_norm_g, [
                jnp.concatenate([w_kv[:, g * D:(g + 1) * D],
                                 w_kv[:, qw + g * D:qw + (g + 1) * D]], axis=1).astype(BF16)
                for g in range(len(DILS))])
    return x
```

```python
import functools
import math

import jax
import jax.numpy as jnp
from jax import lax
from jax.experimental import pallas as pl
from jax.experimental.pallas import tpu as pltpu

D = 1024
F = 2816
H = 8
DH = 128
DILS = (1, 4, 16)
BLK = 128
KW = 31
EPS = 1e-6
NEG = -1e30

T = 512
FC = 256
NC = F // FC
NS = 2 * FC // 128
HALO = 32
RB = 64
GRB = 64
VMEM_LIMIT = 56 * 1024 * 1024

F32 = jnp.float32
BF16 = jnp.bfloat16


def _rms(x, g):
    return x * lax.rsqrt(jnp.mean(x * x, axis=-1, keepdims=True) + EPS) * g


def _rows_at(ref, slab, start, rows):
    return ref[pl.ds(slab, 1, stride=2), pl.ds(start, rows), :][0]


def _const_spec(shape):
    n = len(shape)
    return pl.BlockSpec(shape, lambda *_: (0,) * n, pipeline_mode=pl.Buffered(1))


def _ffn_kernel(x_ref, gpre_ref, win_ref, dwc_ref, wout_ref, gpost_ref, o_ref,
                h_ref, ubuf_ref, act_ref, y_ref, tail_ref):
    s = pl.program_id(1)

    @pl.when(s == 0)
    def _():
        tail_ref[...] = jnp.zeros_like(tail_ref)

    for rb in range(T // RB):
        rows = slice(rb * RB, (rb + 1) * RB)
        h_ref[rows, :] = _rms(x_ref[rows, :], gpre_ref[...]).astype(BF16)

    def up_half(c, slot, half):
        n0 = half * F + c * FC
        u = jnp.dot(h_ref[...], win_ref[:, n0:n0 + FC], preferred_element_type=F32)
        for k in range(FC // 128):
            sl = slot * NS + half * (FC // 128) + k
            ubuf_ref[sl, 0:8, :] = tail_ref[:, n0 + k * 128:n0 + (k + 1) * 128]
            ubuf_ref[sl, 8:8 + T, :] = u[:, k * 128:(k + 1) * 128]
        tail_ref[:, n0:n0 + FC] = u[T - 8:T, :]

    def gate_rows(c, slot, r0):
        def conv(sl, l0):
            w = dwc_ref[:, l0:l0 + 128]
            return (w[0:1] * _rows_at(ubuf_ref, sl, 6 + r0, GRB)
                    + w[1:2] * _rows_at(ubuf_ref, sl, 7 + r0, GRB)
                    + w[2:3] * ubuf_ref[sl, 8 + r0:8 + r0 + GRB, :]) + w[3:4]

        for k in range(FC // 128):
            a = conv(slot * NS + k, c * FC + k * 128)
            g = conv(slot * NS + FC // 128 + k, F + c * FC + k * 128)
            act_ref[r0:r0 + GRB, c * FC + k * 128:c * FC + (k + 1) * 128] = (
                (g / (1.0 + jnp.exp(-g))) * a).astype(BF16)

    up_half(0, 0, 0)
    up_half(0, 0, 1)
    for c in range(NC):
        slot = c % 2
        if c + 1 < NC:
            up_half(c + 1, 1 - slot, 0)
            up_half(c + 1, 1 - slot, 1)
        for rb in range(T // GRB):
            gate_rows(c, slot, rb * GRB)
    y_ref[...] = jnp.dot(act_ref[...], wout_ref[...], preferred_element_type=F32)
    for rb in range(T // RB):
        rows = slice(rb * RB, (rb + 1) * RB)
        o_ref[rows, :] = x_ref[rows, :] + _rms(y_ref[rows, :], gpost_ref[...])


def _ffn(x, gpre, w_in, dw, dwb, w_out, gpost):
    b, s, _ = x.shape
    win_c = w_in.astype(BF16)
    dwc = jnp.concatenate([dw, dwb[None, :], jnp.zeros((4, 2 * F), F32)], axis=0)
    wout_c = w_out.astype(BF16)
    return pl.pallas_call(
        _ffn_kernel,
        out_shape=jax.ShapeDtypeStruct(x.shape, F32),
        grid=(b, s // T),
        in_specs=[
            pl.BlockSpec((None, T, D), lambda i, j: (i, j, 0)),
            _const_spec((1, D)),
            _const_spec((D, 2 * F)),
            _const_spec((8, 2 * F)),
            _const_spec((F, D)),
            _const_spec((1, D)),
        ],
        out_specs=pl.BlockSpec((None, T, D), lambda i, j: (i, j, 0)),
        scratch_shapes=[
            pltpu.VMEM((T, D), BF16),
            pltpu.VMEM((2 * NS, T + 8, 128), F32),
            pltpu.VMEM((T, F), BF16),
            pltpu.VMEM((T, D), F32),
            pltpu.VMEM((8, 2 * F), F32),
        ],
        compiler_params=pltpu.CompilerParams(
            dimension_semantics=("arbitrary", "arbitrary"),
            vmem_limit_bytes=VMEM_LIMIT),
        name="conv_ffn",
    )(x, gpre.reshape(1, D), win_c, dwc, wout_c, gpost.reshape(1, D))


def _conf_kernel(x_ref, gpre_ref, win_ref, bin_ref, dw_ref, dwb_ref, lng_ref, lnb_ref,
                 wout_ref, bout_ref, gpost_ref, o_ref, h_ref, gbuf_ref, cbuf_ref, sbuf_ref):
    s = pl.program_id(1)

    @pl.when(s == 0)
    def _():
        gbuf_ref[:, 0:HALO, :] = jnp.zeros((D // 128, HALO, 128), F32)

    @pl.when(s > 0)
    def _():
        gbuf_ref[:, 0:HALO, :] = gbuf_ref[:, T:T + HALO, :]

    h_ref[...] = _rms(x_ref[...], gpre_ref[...]).astype(BF16)
    CB = 256
    for cb in range(D // CB):
        c0 = cb * CB
        ua = jnp.dot(h_ref[...], win_ref[:, c0:c0 + CB], preferred_element_type=F32)
        ug = jnp.dot(h_ref[...], win_ref[:, D + c0:D + c0 + CB], preferred_element_type=F32)
        ua = ua + bin_ref[:, c0:c0 + CB]
        ug = ug + bin_ref[:, D + c0:D + c0 + CB]
        glu = ua * (1.0 / (1.0 + jnp.exp(-ug)))
        for k in range(CB // 128):
            gbuf_ref[cb * (CB // 128) + k, HALO:HALO + T, :] = glu[:, k * 128:(k + 1) * 128]

    off = HALO - (KW - 1)
    for lb in range(D // 128):
        l0 = lb * 128

        for rb in range(T // RB):
            r0 = rb * RB
            acc = jnp.broadcast_to(dwb_ref[:, l0:l0 + 128], (RB, 128))
            for j in range(KW):
                acc = acc + dw_ref[j:j + 1, l0:l0 + 128] * _rows_at(gbuf_ref, lb, r0 + off + j, RB)
            cbuf_ref[r0:r0 + RB, l0:l0 + 128] = acc

    for rb in range(T // RB):
        r0 = rb * RB
        c = cbuf_ref[r0:r0 + RB, :]
        mu = jnp.mean(c, axis=-1, keepdims=True)
        cc = c - mu
        var = jnp.mean(cc * cc, axis=-1, keepdims=True)
        y = cc * lax.rsqrt(var + EPS) * lng_ref[...] + lnb_ref[...]
        sbuf_ref[r0:r0 + RB, :] = (y * (1.0 / (1.0 + jnp.exp(-y)))).astype(BF16)

    y = jnp.dot(sbuf_ref[...], wout_ref[...], preferred_element_type=F32) + bout_ref[...]
    o_ref[...] = x_ref[...] + _rms(y, gpost_ref[...])


def _conformer(x, gpre, w_in, b_in, dw, dwb, lng, lnb, w_out, b_out, gpost):
    b, s, _ = x.shape
    dwp = jnp.concatenate([dw, jnp.zeros((32 - KW, D), F32)], axis=0)
    return pl.pallas_call(
        _conf_kernel,
        out_shape=jax.ShapeDtypeStruct(x.shape, F32),
        grid=(b, s // T),
        in_specs=[
            pl.BlockSpec((None, T, D), lambda i, j: (i, j, 0)),
            _const_spec((1, D)),
            _const_spec((D, 2 * D)),
            _const_spec((1, 2 * D)),
            _const_spec((32, D)),
            _const_spec((1, D)),
            _const_spec((1, D)),
            _const_spec((1, D)),
            _const_spec((D, D)),
            _const_spec((1, D)),
            _const_spec((1, D)),
        ],
        out_specs=pl.BlockSpec((None, T, D), lambda i, j: (i, j, 0)),
        scratch_shapes=[
            pltpu.VMEM((T, D), BF16),
            pltpu.VMEM((D // 128, T + HALO, 128), F32),
            pltpu.VMEM((T, D), F32),
            pltpu.VMEM((T, D), BF16),
        ],
        compiler_params=pltpu.CompilerParams(
            dimension_semantics=("arbitrary", "arbitrary"),
            vmem_limit_bytes=VMEM_LIMIT),
        name="conformer",
    )(x, gpre.reshape(1, D), w_in.astype(BF16), b_in.reshape(1, 2 * D), dwp, dwb.reshape(1, D),
      lng.reshape(1, D), lnb.reshape(1, D), w_out.astype(BF16), b_out.reshape(1, D),
      gpost.reshape(1, D))


NB = 512
HG = 8


def _proj_kernel(x_ref, g_ref, *refs):
    nbr = len(DILS)
    w_refs, o_refs = refs[:nbr], refs[nbr:2 * nbr]
    h_ref, slab_ref, tmp_ref = refs[2 * nbr:]
    h_ref[...] = _rms(x_ref[...], g_ref[...]).astype(BF16)
    for w_ref, o_ref, dil in zip(w_refs, o_refs, DILS):
        for nb in range(w_ref.shape[1] // NB):
            res = jnp.dot(h_ref[...], w_ref[:, nb * NB:(nb + 1) * NB], preferred_element_type=F32)
            if dil == 1:
                o_ref[0, :, nb * NB:(nb + 1) * NB] = res.astype(BF16)
                continue
            for k in range(NB // 128):
                slab_ref[k] = res[:, k * 128:(k + 1) * 128]
            for k in range(NB // 128):
                c0 = nb * NB + k * 128
                if dil == 4:
                    for r in range(4):
                        o_ref[r, :, c0:c0 + 128] = (
                            slab_ref[k, pl.ds(r, T // 4, stride=4), :].astype(BF16))
                else:
                    for q in range(4):
                        tmp_ref[q] = slab_ref[k, pl.ds(q, T // 4, stride=4), :]
                    for q in range(4):
                        for p in range(4):
                            o_ref[q + 4 * p, :, c0:c0 + 128] = (
                                tmp_ref[q, pl.ds(p, T // 16, stride=4), :].astype(BF16))


def _proj(x, g, ws):
    assert DILS == (1, 4, 16)
    b, s, _ = x.shape
    n = ws[0].shape[1]
    return pl.pallas_call(
        _proj_kernel,
        out_shape=[jax.ShapeDtypeStruct((b, dil, s // dil, n), BF16) for dil in DILS],
        grid=(b, s // T),
        in_specs=[pl.BlockSpec((None, T, D), lambda i, j: (i, j, 0)), _const_spec((1, D))]
        + [_const_spec((D, n))] * len(DILS),
        out_specs=[pl.BlockSpec((None, dil, T // dil, n), lambda i, j: (i, 0, j, 0))
                   for dil in DILS],
        scratch_shapes=[pltpu.VMEM((T, D), BF16), pltpu.VMEM((NB // 128, T, 128), F32),
                        pltpu.VMEM((4, T // 4, 128), F32)],
        compiler_params=pltpu.CompilerParams(
            dimension_semantics=("arbitrary", "arbitrary"),
            vmem_limit_bytes=VMEM_LIMIT),
        name=f"proj_n{n}",
    )(x, g.reshape(1, D), *ws)


def _attn_kernel(*refs, dil, n_seq, nq, has_halo):
    if has_halo:
        q_ref, kv_ref, halo_ref, o_ref, lse_ref = refs
    else:
        q_ref, kv_ref, o_ref, lse_ref = refs
    step = pl.program_id(1)
    first_tile = step == 0

    def token_rows(i, n):
        if dil == 1:
            return pl.ds(n * BLK, BLK)
        if has_halo:
            return pl.ds(i, BLK, stride=dil)
        return pl.ds(step * n_seq + i, BLK, stride=dil)

    row = lax.broadcasted_iota(jnp.int32, (BLK, BLK), 0)
    col = lax.broadcasted_iota(jnp.int32, (BLK, BLK), 1)
    bias_cur = jnp.where(col <= row, 0.0, NEG)
    bias_prev = jnp.where(col >= row, 0.0, NEG)
    bias_halo = jnp.where(col >= row + jnp.where(first_tile, BLK, 0), 0.0, NEG)
    scale = 1.0 / math.sqrt(DH)
    exp2_scale = scale * math.log2(math.e)
    dn = (((1,), (1,)), ((), ()))

    def scores(i, n, h):
        r0, c0 = n * BLK, h * DH
        q = q_ref[i, r0:r0 + BLK, c0:c0 + DH]
        if n > 0:
            k = kv_ref[i, r0 - BLK:r0 + BLK, c0:c0 + DH]
            s = lax.dot_general(q, k, dn, preferred_element_type=F32)
            return s + jnp.concatenate([bias_prev, bias_cur], axis=1)
        s_cur = lax.dot_general(q, kv_ref[i, r0:r0 + BLK, c0:c0 + DH], dn,
                                preferred_element_type=F32) + bias_cur
        if not has_halo:
            return s_cur
        s_prev = lax.dot_general(q, halo_ref[i, :, c0:c0 + DH], dn,
                                 preferred_element_type=F32) + bias_halo
        return jnp.concatenate([s_prev, s_cur], axis=1)

    def weighted_values(i, n, h, p):
        r0, c0 = n * BLK, D + h * DH
        if n > 0:
            return jnp.dot(p, kv_ref[i, r0 - BLK:r0 + BLK, c0:c0 + DH], preferred_element_type=F32)
        if not has_halo:
            return jnp.dot(p, kv_ref[i, r0:r0 + BLK, c0:c0 + DH], preferred_element_type=F32)
        return (jnp.dot(p[:, :BLK], halo_ref[i, :, c0:c0 + DH], preferred_element_type=F32)
                + jnp.dot(p[:, BLK:], kv_ref[i, r0:r0 + BLK, c0:c0 + DH],
                          preferred_element_type=F32))

    items = [(i, n, h) for i in range(n_seq) for n in range(nq) for h in range(H)]
    lse_tiles = {}
    for g0 in range(0, len(items), HG):
        group = items[g0:g0 + HG]
        s = [scores(*it) for it in group]
        m = [jnp.max(x, axis=-1, keepdims=True) for x in s]
        p = [jnp.exp2((x - mx) * exp2_scale) for x, mx in zip(s, m)]
        den = [jnp.sum(x, axis=-1, keepdims=True) for x in p]
        out = [weighted_values(*it, x.astype(BF16)) for it, x in zip(group, p)]
        for (i, n, h), o, dd, mx in zip(group, out, den, m):
            o_ref[h, token_rows(i, n), :] = o / dd
            tile = lse_tiles.get((i, n), jnp.zeros((BLK, BLK), F32))
            tile = jnp.where(col == h, mx * scale + jnp.log(dd), tile)
            if h == H - 1:
                lse_ref[token_rows(i, n), :] = tile
            else:
                lse_tiles[(i, n)] = tile


def _attn(q, kv, dil):
    b, _, L, _ = q.shape
    s = L * dil
    n_seq = min(dil, T // BLK)
    nq = T // (n_seq * BLK)
    has_halo = L > nq * BLK
    if has_halo:
        seq_map = lambda i, j: (i, 0, j, 0)
        out_rows = T
        out_map = lambda i, j: (i, 0, j, 0)
        lse_map = lambda i, j: (i, j, 0)
    else:
        seq_map = lambda i, j: (i, j, 0, 0)
        out_rows = s
        out_map = lambda i, j: (i, 0, 0, 0)
        lse_map = lambda i, j: (i, 0, 0)
    in_specs = [pl.BlockSpec((None, n_seq, nq * BLK, D), seq_map),
                pl.BlockSpec((None, n_seq, nq * BLK, 2 * D), seq_map)]
    args = [q, kv]
    if has_halo:
        in_specs.append(pl.BlockSpec(
            (None, n_seq, BLK, 2 * D), lambda i, j: (i, 0, jnp.maximum(j * nq - 1, 0), 0)))
        args.append(kv)
    return pl.pallas_call(
        functools.partial(_attn_kernel, dil=dil, n_seq=n_seq, nq=nq, has_halo=has_halo),
        out_shape=(jax.ShapeDtypeStruct((b, H, s, DH), F32),
                   jax.ShapeDtypeStruct((b, s, BLK), F32)),
        grid=(b, s // T),
        in_specs=in_specs,
        out_specs=(pl.BlockSpec((None, H, out_rows, DH), out_map),
                   pl.BlockSpec((None, out_rows, BLK), lse_map)),
        compiler_params=pltpu.CompilerParams(
            dimension_semantics=("arbitrary", "arbitrary"),
            vmem_limit_bytes=VMEM_LIMIT),
        name=f"attn_d{dil}",
    )(*args)


def _merge_kernel(x_ref, o0_ref, o1_ref, o2_ref, l0_ref, l1_ref, l2_ref, wo_ref, gpost_ref,
                  out_ref, mbuf_ref):
    o_refs = (o0_ref, o1_ref, o2_ref)
    l_refs = (l0_ref, l1_ref, l2_ref)
    for rb in range(T // RB):
        r0 = rb * RB
        ls = [l[r0:r0 + RB, :] for l in l_refs]
        mx = jnp.maximum(jnp.maximum(ls[0], ls[1]), ls[2])
        es = [jnp.exp(l - mx) for l in ls]
        tot = es[0] + es[1] + es[2]
        ws = [e / tot for e in es]
        for h in range(H):
            c0 = h * DH
            acc = ws[0][:, h:h + 1] * o_refs[0][h, r0:r0 + RB, :]
            acc = acc + ws[1][:, h:h + 1] * o_refs[1][h, r0:r0 + RB, :]
            acc = acc + ws[2][:, h:h + 1] * o_refs[2][h, r0:r0 + RB, :]
            mbuf_ref[r0:r0 + RB, c0:c0 + DH] = acc.astype(BF16)
    y = jnp.dot(mbuf_ref[...], wo_ref[...], preferred_element_type=F32)
    out_ref[...] = x_ref[...] + _rms(y, gpost_ref[...])


def _merge(x, outs, lses, w_o, gpost):
    b, s, _ = x.shape
    tile = lambda w: pl.BlockSpec((None, T, w), lambda i, j: (i, j, 0))
    heads = pl.BlockSpec((None, H, T, DH), lambda i, j: (i, 0, j, 0))
    return pl.pallas_call(
        _merge_kernel,
        out_shape=jax.ShapeDtypeStruct(x.shape, F32),
        grid=(b, s // T),
        in_specs=[tile(D)] + [heads] * 3 + [tile(BLK)] * 3 + [_const_spec((D, D)), _const_spec((1, D))],
        out_specs=tile(D),
        scratch_shapes=[pltpu.VMEM((T, D), BF16)],
        compiler_params=pltpu.CompilerParams(
            dimension_semantics=("arbitrary", "arbitrary"),
            vmem_limit_bytes=VMEM_LIMIT),
        name="merge",
    )(x, *outs, *lses, w_o.astype(BF16), gpost.reshape(1, D))


def kernel(x, mix_pre_g, mix_post_g, ffn_pre_g, ffn_post_g, cm_w_in, cm_b_in, cm_dw, cm_dw_b,
           cm_ln_g, cm_ln_b, cm_w_out, cm_b_out, kv_norm_g, w_kv, w_q, w_o, ffn_w_in, ffn_dw,
           ffn_dw_b, ffn_w_out):
    n_a = cm_w_in.shape[0]
    depth = ffn_w_in.shape[0]
    qw = len(DILS) * D
    kvs = None
    for i in range(depth):
        if i < n_a:
            x = _conformer(x, mix_pre_g[i], cm_w_in[i], cm_b_in[i], cm_dw[i], cm_dw_b[i],
                           cm_ln_g[i], cm_ln_b[i], cm_w_out[i], cm_b_out[i], mix_post_g[i])
        else:
            j = i - n_a
            qs = _proj(x, mix_pre_g[i],
                       [w_q[j][:, g * D:(g + 1) * D].astype(BF16) for g in range(len(DILS))])
            outs, lses = [], []
            for g, dil in enumerate(DILS):
                o, l = _attn(qs[g], kvs[g], dil)
                outs.append(o)
                lses.append(l)
            x = _merge(x, outs, lses, w_o[j], mix_post_g[i])
        x = _ffn(x, ffn_pre_g[i], ffn_w_in[i], ffn_dw[i], ffn_dw_b[i], ffn_w_out[i], ffn_post_g[i])
        if i == n_a - 1:
            kvs = _proj(x, kv_norm_g, [
                jnp.concatenate([w_kv[:, g * D:(g + 1) * D],
                                 w_kv[:, qw + g * D:qw + (g + 1) * D]], axis=1).astype(BF16)
                for g in range(len(DILS))])
    return x
```

```python
import functools
import math

import jax
import jax.numpy as jnp
from jax import lax
from jax.experimental import pallas as pl
from jax.experimental.pallas import tpu as pltpu

D = 1024
F = 2816
H = 8
DH = 128
DILS = (1, 4, 16)
BLK = 128
KW = 31
EPS = 1e-6
NEG = -1e30

T = 512
TA = 2048
FC = 256
NC = F // FC
NS = 2 * FC // 128
HALO = 32
RB = 64
GRB = 64
VMEM_LIMIT = 56 * 1024 * 1024

F32 = jnp.float32
BF16 = jnp.bfloat16


def _rms(x, g):
    return x * lax.rsqrt(jnp.mean(x * x, axis=-1, keepdims=True) + EPS) * g


def _rows_at(ref, slab, start, rows):
    return ref[pl.ds(slab, 1, stride=2), pl.ds(start, rows), :][0]


def _const_spec(shape):
    n = len(shape)
    if math.prod(shape) < 256 * 1024:
        return pl.BlockSpec(shape, lambda *_: (0,) * n)
    return pl.BlockSpec(shape, lambda *_: (0,) * n, pipeline_mode=pl.Buffered(1))


def _sub_tiles(n_sub, tile_fn, row_refs, *rest):
    def body(k, carry):
        rows = pl.ds(pl.multiple_of(k * T, T), T)
        tile_fn(*[r.at[rows, :] for r in row_refs], *rest)
        return carry
    lax.fori_loop(0, n_sub, body, 0)


def _ffn_kernel(x_ref, gpre_ref, win_ref, dwc_ref, wout_ref, gpost_ref, o_ref,
                h_ref, ubuf_ref, act_ref, y_ref, tail_ref, *, n_sub):
    @pl.when(pl.program_id(1) == 0)
    def _():
        tail_ref[...] = jnp.zeros_like(tail_ref)

    _sub_tiles(n_sub, _ffn_tile, (x_ref, o_ref), gpre_ref, win_ref, dwc_ref, wout_ref, gpost_ref,
               h_ref, ubuf_ref, act_ref, y_ref, tail_ref)


def _ffn_tile(x_ref, o_ref, gpre_ref, win_ref, dwc_ref, wout_ref, gpost_ref,
              h_ref, ubuf_ref, act_ref, y_ref, tail_ref):
    for rb in range(T // RB):
        rows = slice(rb * RB, (rb + 1) * RB)
        h_ref[rows, :] = _rms(x_ref[rows, :], gpre_ref[...]).astype(BF16)

    def up_half(c, slot, half):
        n0 = half * F + c * FC
        u = jnp.dot(h_ref[...], win_ref[:, n0:n0 + FC], preferred_element_type=F32)
        for k in range(FC // 128):
            sl = slot * NS + half * (FC // 128) + k
            ubuf_ref[sl, 0:8, :] = tail_ref[:, n0 + k * 128:n0 + (k + 1) * 128]
            ubuf_ref[sl, 8:8 + T, :] = u[:, k * 128:(k + 1) * 128]
        tail_ref[:, n0:n0 + FC] = u[T - 8:T, :]

    def gate_rows(c, slot, r0):
        def conv(sl, l0):
            w = dwc_ref[:, l0:l0 + 128]
            return (w[0:1] * _rows_at(ubuf_ref, sl, 6 + r0, GRB)
                    + w[1:2] * _rows_at(ubuf_ref, sl, 7 + r0, GRB)
                    + w[2:3] * ubuf_ref[sl, 8 + r0:8 + r0 + GRB, :]) + w[3:4]

        for k in range(FC // 128):
            a = conv(slot * NS + k, c * FC + k * 128)
            g = conv(slot * NS + FC // 128 + k, F + c * FC + k * 128)
            act_ref[r0:r0 + GRB, c * FC + k * 128:c * FC + (k + 1) * 128] = (
                (g / (1.0 + jnp.exp(-g))) * a).astype(BF16)

    up_half(0, 0, 0)
    up_half(0, 0, 1)
    for c in range(NC):
        slot = c % 2
        if c + 1 < NC:
            up_half(c + 1, 1 - slot, 0)
            up_half(c + 1, 1 - slot, 1)
        for rb in range(T // GRB):
            gate_rows(c, slot, rb * GRB)
    y_ref[...] = jnp.dot(act_ref[...], wout_ref[...], preferred_element_type=F32)
    for rb in range(T // RB):
        rows = slice(rb * RB, (rb + 1) * RB)
        o_ref[rows, :] = x_ref[rows, :] + _rms(y_ref[rows, :], gpost_ref[...])


def _ffn(x, gpre, w_in, dw, dwb, w_out, gpost):
    b, s, _ = x.shape
    win_c = w_in.astype(BF16)
    dwc = jnp.concatenate([dw, dwb[None, :], jnp.zeros((4, 2 * F), F32)], axis=0)
    wout_c = w_out.astype(BF16)
    n_sub = 2
    return pl.pallas_call(
        functools.partial(_ffn_kernel, n_sub=n_sub),
        out_shape=jax.ShapeDtypeStruct(x.shape, F32),
        grid=(b, s // (n_sub * T)),
        in_specs=[
            pl.BlockSpec((None, n_sub * T, D), lambda i, j: (i, j, 0)),
            _const_spec((1, D)),
            _const_spec((D, 2 * F)),
            _const_spec((8, 2 * F)),
            _const_spec((F, D)),
            _const_spec((1, D)),
        ],
        out_specs=pl.BlockSpec((None, n_sub * T, D), lambda i, j: (i, j, 0)),
        scratch_shapes=[
            pltpu.VMEM((T, D), BF16),
            pltpu.VMEM((2 * NS, T + 8, 128), F32),
            pltpu.VMEM((T, F), BF16),
            pltpu.VMEM((T, D), F32),
            pltpu.VMEM((8, 2 * F), F32),
        ],
        compiler_params=pltpu.CompilerParams(
            dimension_semantics=("arbitrary", "arbitrary"),
            vmem_limit_bytes=VMEM_LIMIT),
        name="conv_ffn",
    )(x, gpre.reshape(1, D), win_c, dwc, wout_c, gpost.reshape(1, D))


def _conf_kernel(x_ref, gpre_ref, win_ref, bin_ref, dw_ref, dwb_ref, lng_ref, lnb_ref,
                 wout_ref, bout_ref, gpost_ref, o_ref, h_ref, gbuf_ref, cbuf_ref, sbuf_ref,
                 *, n_sub):
    @pl.when(pl.program_id(1) == 0)
    def _():
        gbuf_ref[:, T:T + HALO, :] = jnp.zeros((D // 128, HALO, 128), F32)

    _sub_tiles(n_sub, _conf_tile, (x_ref, o_ref), gpre_ref, win_ref, bin_ref, dw_ref, dwb_ref,
               lng_ref, lnb_ref, wout_ref, bout_ref, gpost_ref, h_ref, gbuf_ref, cbuf_ref, sbuf_ref)


def _conf_tile(x_ref, o_ref, gpre_ref, win_ref, bin_ref, dw_ref, dwb_ref, lng_ref, lnb_ref,
               wout_ref, bout_ref, gpost_ref, h_ref, gbuf_ref, cbuf_ref, sbuf_ref):
    gbuf_ref[:, 0:HALO, :] = gbuf_ref[:, T:T + HALO, :]
    h_ref[...] = _rms(x_ref[...], gpre_ref[...]).astype(BF16)
    CB = 256
    for cb in range(D // CB):
        c0 = cb * CB
        ua = jnp.dot(h_ref[...], win_ref[:, c0:c0 + CB], preferred_element_type=F32)
        ug = jnp.dot(h_ref[...], win_ref[:, D + c0:D + c0 + CB], preferred_element_type=F32)
        ua = ua + bin_ref[:, c0:c0 + CB]
        ug = ug + bin_ref[:, D + c0:D + c0 + CB]
        glu = ua * (1.0 / (1.0 + jnp.exp(-ug)))
        for k in range(CB // 128):
            gbuf_ref[cb * (CB // 128) + k, HALO:HALO + T, :] = glu[:, k * 128:(k + 1) * 128]

    off = HALO - (KW - 1)
    for lb in range(D // 128):
        l0 = lb * 128

        for rb in range(T // RB):
            r0 = rb * RB
            acc = jnp.broadcast_to(dwb_ref[:, l0:l0 + 128], (RB, 128))
            for j in range(KW):
                acc = acc + dw_ref[j:j + 1, l0:l0 + 128] * _rows_at(gbuf_ref, lb, r0 + off + j, RB)
            cbuf_ref[r0:r0 + RB, l0:l0 + 128] = acc

    for rb in range(T // RB):
        r0 = rb * RB
        c = cbuf_ref[r0:r0 + RB, :]
        mu = jnp.mean(c, axis=-1, keepdims=True)
        cc = c - mu
        var = jnp.mean(cc * cc, axis=-1, keepdims=True)
        y = cc * lax.rsqrt(var + EPS) * lng_ref[...] + lnb_ref[...]
        sbuf_ref[r0:r0 + RB, :] = (y * (1.0 / (1.0 + jnp.exp(-y)))).astype(BF16)

    y = jnp.dot(sbuf_ref[...], wout_ref[...], preferred_element_type=F32) + bout_ref[...]
    o_ref[...] = x_ref[...] + _rms(y, gpost_ref[...])


def _conformer(x, gpre, w_in, b_in, dw, dwb, lng, lnb, w_out, b_out, gpost):
    b, s, _ = x.shape
    dwp = jnp.concatenate([dw, jnp.zeros((32 - KW, D), F32)], axis=0)
    n_sub = 4
    return pl.pallas_call(
        functools.partial(_conf_kernel, n_sub=n_sub),
        out_shape=jax.ShapeDtypeStruct(x.shape, F32),
        grid=(b, s // (n_sub * T)),
        in_specs=[
            pl.BlockSpec((None, n_sub * T, D), lambda i, j: (i, j, 0)),
            _const_spec((1, D)),
            _const_spec((D, 2 * D)),
            _const_spec((1, 2 * D)),
            _const_spec((32, D)),
            _const_spec((1, D)),
            _const_spec((1, D)),
            _const_spec((1, D)),
            _const_spec((D, D)),
            _const_spec((1, D)),
            _const_spec((1, D)),
        ],
        out_specs=pl.BlockSpec((None, n_sub * T, D), lambda i, j: (i, j, 0)),
        scratch_shapes=[
            pltpu.VMEM((T, D), BF16),
            pltpu.VMEM((D // 128, T + HALO, 128), F32),
            pltpu.VMEM((T, D), F32),
            pltpu.VMEM((T, D), BF16),
        ],
        compiler_params=pltpu.CompilerParams(
            dimension_semantics=("arbitrary", "arbitrary"),
            vmem_limit_bytes=VMEM_LIMIT),
        name="conformer",
    )(x, gpre.reshape(1, D), w_in.astype(BF16), b_in.reshape(1, 2 * D), dwp, dwb.reshape(1, D),
      lng.reshape(1, D), lnb.reshape(1, D), w_out.astype(BF16), b_out.reshape(1, D),
      gpost.reshape(1, D))


NB = 512
HG = 8


def _proj_kernel(x_ref, g_ref, *refs, n_sub):
    nbr = len(DILS)
    w_refs, o_refs = refs[:nbr], refs[nbr:2 * nbr]
    scratch = refs[2 * nbr:]

    def body(k, carry):
        xs = x_ref.at[pl.ds(pl.multiple_of(k * T, T), T), :]
        outs = [o.at[:, pl.ds(pl.multiple_of(k * (T // dil), T // dil), T // dil), :]
                for o, dil in zip(o_refs, DILS)]
        _proj_tile(xs, g_ref, w_refs, outs, *scratch)
        return carry

    lax.fori_loop(0, n_sub, body, 0)


def _proj_tile(x_ref, g_ref, w_refs, o_refs, h_ref, slab_ref, tmp_ref):
    h_ref[...] = _rms(x_ref[...], g_ref[...]).astype(BF16)
    for w_ref, o_ref, dil in zip(w_refs, o_refs, DILS):
        for nb in range(w_ref.shape[1] // NB):
            res = jnp.dot(h_ref[...], w_ref[:, nb * NB:(nb + 1) * NB], preferred_element_type=F32)
            if dil == 1:
                o_ref[0, :, nb * NB:(nb + 1) * NB] = res.astype(BF16)
                continue
            for k in range(NB // 128):
                slab_ref[k] = res[:, k * 128:(k + 1) * 128]
            for k in range(NB // 128):
                c0 = nb * NB + k * 128
                if dil == 4:
                    for r in range(4):
                        o_ref[r, :, c0:c0 + 128] = (
                            slab_ref[k, pl.ds(r, T // 4, stride=4), :].astype(BF16))
                else:
                    for q in range(4):
                        tmp_ref[q] = slab_ref[k, pl.ds(q, T // 4, stride=4), :]
                    for q in range(4):
                        for p in range(4):
                            o_ref[q + 4 * p, :, c0:c0 + 128] = (
                                tmp_ref[q, pl.ds(p, T // 16, stride=4), :].astype(BF16))


def _proj(x, g, ws):
    assert DILS == (1, 4, 16)
    b, s, _ = x.shape
    n = ws[0].shape[1]
    n_sub = 2
    tb = n_sub * T
    return pl.pallas_call(
        functools.partial(_proj_kernel, n_sub=n_sub),
        out_shape=[jax.ShapeDtypeStruct((b, dil, s // dil, n), BF16) for dil in DILS],
        grid=(b, s // tb),
        in_specs=[pl.BlockSpec((None, tb, D), lambda i, j: (i, j, 0)), _const_spec((1, D))]
        + [_const_spec((D, n))] * len(DILS),
        out_specs=[pl.BlockSpec((None, dil, tb // dil, n), lambda i, j: (i, 0, j, 0))
                   for dil in DILS],
        scratch_shapes=[pltpu.VMEM((T, D), BF16), pltpu.VMEM((NB // 128, T, 128), F32),
                        pltpu.VMEM((4, T // 4, 128), F32)],
        compiler_params=pltpu.CompilerParams(
            dimension_semantics=("arbitrary", "arbitrary"),
            vmem_limit_bytes=VMEM_LIMIT),
        name=f"proj_n{n}",
    )(x, g.reshape(1, D), *ws)


def _attn_kernel(*refs, dil, n_seq, nq, has_halo):
    if has_halo:
        q_ref, kv_ref, halo_ref, o_ref, lse_ref = refs
    else:
        q_ref, kv_ref, o_ref, lse_ref = refs
    step = pl.program_id(1)
    first_tile = step == 0

    def token_rows(i, n):
        if dil == 1:
            return pl.ds(n * BLK, BLK)
        if has_halo:
            return pl.ds(n * BLK * dil + i, BLK, stride=dil)
        return pl.ds(n * BLK * dil + step * n_seq + i, BLK, stride=dil)

    row = lax.broadcasted_iota(jnp.int32, (BLK, BLK), 0)
    col = lax.broadcasted_iota(jnp.int32, (BLK, BLK), 1)
    bias_cur = jnp.where(col <= row, 0.0, NEG)
    bias_prev = jnp.where(col >= row, 0.0, NEG)
    bias_halo = jnp.where(col >= row + jnp.where(first_tile, BLK, 0), 0.0, NEG)
    scale = 1.0 / math.sqrt(DH)
    exp2_scale = scale * math.log2(math.e)
    dn = (((1,), (1,)), ((), ()))

    def scores(i, n, h):
        r0, c0 = n * BLK, h * DH
        q = q_ref[i, r0:r0 + BLK, c0:c0 + DH]
        if n > 0:
            k = kv_ref[i, r0 - BLK:r0 + BLK, c0:c0 + DH]
            s = lax.dot_general(q, k, dn, preferred_element_type=F32)
            return s + jnp.concatenate([bias_prev, bias_cur], axis=1)
        s_cur = lax.dot_general(q, kv_ref[i, r0:r0 + BLK, c0:c0 + DH], dn,
                                preferred_element_type=F32) + bias_cur
        if not has_halo:
            return s_cur
        s_prev = lax.dot_general(q, halo_ref[i, :, c0:c0 + DH], dn,
                                 preferred_element_type=F32) + bias_halo
        return jnp.concatenate([s_prev, s_cur], axis=1)

    def weighted_values(i, n, h, p):
        r0, c0 = n * BLK, D + h * DH
        if n > 0:
            return jnp.dot(p, kv_ref[i, r0 - BLK:r0 + BLK, c0:c0 + DH], preferred_element_type=F32)
        if not has_halo:
            return jnp.dot(p, kv_ref[i, r0:r0 + BLK, c0:c0 + DH], preferred_element_type=F32)
        return (jnp.dot(p[:, :BLK], halo_ref[i, :, c0:c0 + DH], preferred_element_type=F32)
                + jnp.dot(p[:, BLK:], kv_ref[i, r0:r0 + BLK, c0:c0 + DH],
                          preferred_element_type=F32))

    items = [(i, n, h) for i in range(n_seq) for n in range(nq) for h in range(H)]
    lse_tiles = {}
    for g0 in range(0, len(items), HG):
        group = items[g0:g0 + HG]
        s = [scores(*it) for it in group]
        m = [jnp.max(x, axis=-1, keepdims=True) for x in s]
        p = [jnp.exp2((x - mx) * exp2_scale) for x, mx in zip(s, m)]
        den = [jnp.sum(x, axis=-1, keepdims=True) for x in p]
        out = [weighted_values(*it, x.astype(BF16)) for it, x in zip(group, p)]
        for (i, n, h), o, dd, mx in zip(group, out, den, m):
            o_ref[h, token_rows(i, n), :] = o / dd
            tile = lse_tiles.get((i, n), jnp.zeros((BLK, BLK), F32))
            tile = jnp.where(col == h, mx * scale + jnp.log(dd), tile)
            if h == H - 1:
                lse_ref[token_rows(i, n), :] = tile
            else:
                lse_tiles[(i, n)] = tile


def _attn(q, kv, dil):
    b, _, L, _ = q.shape
    s = L * dil
    n_seq = min(dil, TA // BLK)
    nq = TA // (n_seq * BLK)
    has_halo = L > nq * BLK
    if has_halo:
        seq_map = lambda i, j: (i, 0, j, 0)
        out_rows = TA
        out_map = lambda i, j: (i, 0, j, 0)
        lse_map = lambda i, j: (i, j, 0)
    else:
        seq_map = lambda i, j: (i, j, 0, 0)
        out_rows = s
        out_map = lambda i, j: (i, 0, 0, 0)
        lse_map = lambda i, j: (i, 0, 0)
    in_specs = [pl.BlockSpec((None, n_seq, nq * BLK, D), seq_map),
                pl.BlockSpec((None, n_seq, nq * BLK, 2 * D), seq_map)]
    args = [q, kv]
    if has_halo:
        in_specs.append(pl.BlockSpec(
            (None, n_seq, BLK, 2 * D), lambda i, j: (i, 0, jnp.maximum(j * nq - 1, 0), 0)))
        args.append(kv)
    return pl.pallas_call(
        functools.partial(_attn_kernel, dil=dil, n_seq=n_seq, nq=nq, has_halo=has_halo),
        out_shape=(jax.ShapeDtypeStruct((b, H, s, DH), F32),
                   jax.ShapeDtypeStruct((b, s, BLK), F32)),
        grid=(b, s // TA),
        in_specs=in_specs,
        out_specs=(pl.BlockSpec((None, H, out_rows, DH), out_map),
                   pl.BlockSpec((None, out_rows, BLK), lse_map)),
        compiler_params=pltpu.CompilerParams(
            dimension_semantics=("arbitrary", "arbitrary"),
            vmem_limit_bytes=VMEM_LIMIT),
        name=f"attn_d{dil}",
    )(*args)


def _merge_kernel(x_ref, o0_ref, o1_ref, o2_ref, l0_ref, l1_ref, l2_ref, wo_ref, gpost_ref,
                  out_ref, mbuf_ref, *, n_sub):
    def body(k, carry):
        rows = pl.ds(pl.multiple_of(k * T, T), T)
        _merge_tile(x_ref.at[rows, :], [o.at[:, rows, :] for o in (o0_ref, o1_ref, o2_ref)],
                    [l.at[rows, :] for l in (l0_ref, l1_ref, l2_ref)], wo_ref, gpost_ref,
                    out_ref.at[rows, :], mbuf_ref)
        return carry

    lax.fori_loop(0, n_sub, body, 0)


def _merge_tile(x_ref, o_refs, l_refs, wo_ref, gpost_ref, out_ref, mbuf_ref):
    for rb in range(T // RB):
        r0 = rb * RB
        ls = [l[r0:r0 + RB, :] for l in l_refs]
        mx = jnp.maximum(jnp.maximum(ls[0], ls[1]), ls[2])
        es = [jnp.exp(l - mx) for l in ls]
        tot = es[0] + es[1] + es[2]
        ws = [e / tot for e in es]
        for h in range(H):
            c0 = h * DH
            acc = ws[0][:, h:h + 1] * o_refs[0][h, r0:r0 + RB, :]
            acc = acc + ws[1][:, h:h + 1] * o_refs[1][h, r0:r0 + RB, :]
            acc = acc + ws[2][:, h:h + 1] * o_refs[2][h, r0:r0 + RB, :]
            mbuf_ref[r0:r0 + RB, c0:c0 + DH] = acc.astype(BF16)
    y = jnp.dot(mbuf_ref[...], wo_ref[...], preferred_element_type=F32)
    out_ref[...] = x_ref[...] + _rms(y, gpost_ref[...])


def _merge(x, outs, lses, w_o, gpost):
    b, s, _ = x.shape
    n_sub = 2
    tb = n_sub * T
    tile = lambda w: pl.BlockSpec((None, tb, w), lambda i, j: (i, j, 0))
    heads = pl.BlockSpec((None, H, tb, DH), lambda i, j: (i, 0, j, 0))
    return pl.pallas_call(
        functools.partial(_merge_kernel, n_sub=n_sub),
        out_shape=jax.ShapeDtypeStruct(x.shape, F32),
        grid=(b, s // tb),
        in_specs=[tile(D)] + [heads] * 3 + [tile(BLK)] * 3 + [_const_spec((D, D)), _const_spec((1, D))],
        out_specs=tile(D),
        scratch_shapes=[pltpu.VMEM((T, D), BF16)],
        compiler_params=pltpu.CompilerParams(
            dimension_semantics=("arbitrary", "arbitrary"),
            vmem_limit_bytes=VMEM_LIMIT),
        name="merge",
    )(x, *outs, *lses, w_o.astype(BF16), gpost.reshape(1, D))


def kernel(x, mix_pre_g, mix_post_g, ffn_pre_g, ffn_post_g, cm_w_in, cm_b_in, cm_dw, cm_dw_b,
           cm_ln_g, cm_ln_b, cm_w_out, cm_b_out, kv_norm_g, w_kv, w_q, w_o, ffn_w_in, ffn_dw,
           ffn_dw_b, ffn_w_out):
    n_a = cm_w_in.shape[0]
    depth = ffn_w_in.shape[0]
    qw = len(DILS) * D
    kvs = None
    for i in range(depth):
        if i < n_a:
            x = _conformer(x, mix_pre_g[i], cm_w_in[i], cm_b_in[i], cm_dw[i], cm_dw_b[i],
                           cm_ln_g[i], cm_ln_b[i], cm_w_out[i], cm_b_out[i], mix_post_g[i])
        else:
            j = i - n_a
            qs = _proj(x, mix_pre_g[i],
                       [w_q[j][:, g * D:(g + 1) * D].astype(BF16) for g in range(len(DILS))])
            outs, lses = [], []
            for g, dil in enumerate(DILS):
                o, l = _attn(qs[g], kvs[g], dil)
                outs.append(o)
                lses.append(l)
            x = _merge(x, outs, lses, w_o[j], mix_post_g[i])
        x = _ffn(x, ffn_pre_g[i], ffn_w_in[i], ffn_dw[i], ffn_dw_b[i], ffn_w_out[i], ffn_post_g[i])
        if i == n_a - 1:
            kvs = _proj(x, kv_norm_g, [
                jnp.concatenate([w_kv[:, g * D:(g + 1) * D],
                                 w_kv[:, qw + g * D:qw + (g + 1) * D]], axis=1).astype(BF16)
                for g in range(len(DILS))])
    return x
```

```python
import functools
import math

import jax
import jax.numpy as jnp
from jax import lax
from jax.experimental import pallas as pl
from jax.experimental.pallas import tpu as pltpu

D = 1024
F = 2816
H = 8
DH = 128
DILS = (1, 4, 16)
BLK = 128
KW = 31
EPS = 1e-6
NEG = -1e30

T = 512
FC = 256
NC = F // FC
NS = 2 * FC // 128
HALO = 32
RB = 64
GRB = 64
VMEM_LIMIT = 56 * 1024 * 1024

F32 = jnp.float32
BF16 = jnp.bfloat16


def _exp_neg(x):
    return jnp.exp2(x * (-math.log2(math.e)))


def _rms(x, g):
    return x * lax.rsqrt(jnp.mean(x * x, axis=-1, keepdims=True) + EPS) * g


def _rows_at(ref, slab, start, rows):
    return ref[pl.ds(slab, 1, stride=2), pl.ds(start, rows), :][0]


def _const_spec(shape):
    n = len(shape)
    if math.prod(shape) < 256 * 1024:
        return pl.BlockSpec(shape, lambda *_: (0,) * n)
    return pl.BlockSpec(shape, lambda *_: (0,) * n, pipeline_mode=pl.Buffered(1))


def _sub_tiles(n_sub, tile_fn, row_refs, *rest):
    def body(k, carry):
        rows = pl.ds(pl.multiple_of(k * T, T), T)
        tile_fn(*[r.at[rows, :] for r in row_refs], *rest)
        return carry
    lax.fori_loop(0, n_sub, body, 0)


def _ffn_kernel(x_ref, gpre_ref, win_ref, dwc_ref, wout_ref, gpost_ref, o_ref,
                h_ref, ubuf_ref, act_ref, y_ref, tail_ref, *, n_sub):
    @pl.when(pl.program_id(1) == 0)
    def _():
        tail_ref[...] = jnp.zeros_like(tail_ref)

    _sub_tiles(n_sub, _ffn_tile, (x_ref, o_ref), gpre_ref, win_ref, dwc_ref, wout_ref, gpost_ref,
               h_ref, ubuf_ref, act_ref, y_ref, tail_ref)


def _ffn_tile(x_ref, o_ref, gpre_ref, win_ref, dwc_ref, wout_ref, gpost_ref,
              h_ref, ubuf_ref, act_ref, y_ref, tail_ref):
    for rb in range(T // RB):
        rows = slice(rb * RB, (rb + 1) * RB)
        h_ref[rows, :] = _rms(x_ref[rows, :], gpre_ref[...]).astype(BF16)

    def up_half(c, slot, half):
        n0 = half * F + c * FC
        u = jnp.dot(h_ref[...], win_ref[:, n0:n0 + FC], preferred_element_type=F32)
        for k in range(FC // 128):
            sl = slot * NS + half * (FC // 128) + k
            ubuf_ref[sl, 0:8, :] = tail_ref[:, n0 + k * 128:n0 + (k + 1) * 128]
            ubuf_ref[sl, 8:8 + T, :] = u[:, k * 128:(k + 1) * 128]
        tail_ref[:, n0:n0 + FC] = u[T - 8:T, :]

    def gate_rows(c, slot, r0):
        def conv(sl, l0):
            w = dwc_ref[:, l0:l0 + 128]
            return (w[0:1] * _rows_at(ubuf_ref, sl, 6 + r0, GRB)
                    + w[1:2] * _rows_at(ubuf_ref, sl, 7 + r0, GRB)
                    + w[2:3] * ubuf_ref[sl, 8 + r0:8 + r0 + GRB, :]) + w[3:4]

        for k in range(FC // 128):
            a = conv(slot * NS + k, c * FC + k * 128)
            g = conv(slot * NS + FC // 128 + k, F + c * FC + k * 128)
            act_ref[r0:r0 + GRB, c * FC + k * 128:c * FC + (k + 1) * 128] = (
                (g / (1.0 + _exp_neg(g))) * a).astype(BF16)

    up_half(0, 0, 0)
    up_half(0, 0, 1)
    for c in range(NC):
        slot = c % 2
        if c + 1 < NC:
            up_half(c + 1, 1 - slot, 0)
            up_half(c + 1, 1 - slot, 1)
        for rb in range(T // GRB):
            gate_rows(c, slot, rb * GRB)
    y_ref[...] = jnp.dot(act_ref[...], wout_ref[...], preferred_element_type=F32)
    for rb in range(T // RB):
        rows = slice(rb * RB, (rb + 1) * RB)
        o_ref[rows, :] = x_ref[rows, :] + _rms(y_ref[rows, :], gpost_ref[...])


def _ffn(x, gpre, w_in, dw, dwb, w_out, gpost):
    b, s, _ = x.shape
    win_c = w_in.astype(BF16)
    dwc = jnp.concatenate([dw, dwb[None, :], jnp.zeros((4, 2 * F), F32)], axis=0)
    wout_c = w_out.astype(BF16)
    n_sub = 2
    return pl.pallas_call(
        functools.partial(_ffn_kernel, n_sub=n_sub),
        out_shape=jax.ShapeDtypeStruct(x.shape, F32),
        grid=(b, s // (n_sub * T)),
        in_specs=[
            pl.BlockSpec((None, n_sub * T, D), lambda i, j: (i, j, 0)),
            _const_spec((1, D)),
            _const_spec((D, 2 * F)),
            _const_spec((8, 2 * F)),
            _const_spec((F, D)),
            _const_spec((1, D)),
        ],
        out_specs=pl.BlockSpec((None, n_sub * T, D), lambda i, j: (i, j, 0)),
        scratch_shapes=[
            pltpu.VMEM((T, D), BF16),
            pltpu.VMEM((2 * NS, T + 8, 128), F32),
            pltpu.VMEM((T, F), BF16),
            pltpu.VMEM((T, D), F32),
            pltpu.VMEM((8, 2 * F), F32),
        ],
        compiler_params=pltpu.CompilerParams(
            dimension_semantics=("arbitrary", "arbitrary"),
            vmem_limit_bytes=VMEM_LIMIT),
        name="conv_ffn",
    )(x, gpre.reshape(1, D), win_c, dwc, wout_c, gpost.reshape(1, D))


def _conf_kernel(x_ref, gpre_ref, win_ref, bin_ref, dw_ref, dwb_ref, lng_ref, lnb_ref,
                 wout_ref, bout_ref, gpost_ref, o_ref, h_ref, gbuf_ref, cbuf_ref, sbuf_ref,
                 *, n_sub):
    @pl.when(pl.program_id(1) == 0)
    def _():
        gbuf_ref[:, T:T + HALO, :] = jnp.zeros((D // 128, HALO, 128), F32)

    _sub_tiles(n_sub, _conf_tile, (x_ref, o_ref), gpre_ref, win_ref, bin_ref, dw_ref, dwb_ref,
               lng_ref, lnb_ref, wout_ref, bout_ref, gpost_ref, h_ref, gbuf_ref, cbuf_ref, sbuf_ref)


def _conf_tile(x_ref, o_ref, gpre_ref, win_ref, bin_ref, dw_ref, dwb_ref, lng_ref, lnb_ref,
               wout_ref, bout_ref, gpost_ref, h_ref, gbuf_ref, cbuf_ref, sbuf_ref):
    gbuf_ref[:, 0:HALO, :] = gbuf_ref[:, T:T + HALO, :]
    h_ref[...] = _rms(x_ref[...], gpre_ref[...]).astype(BF16)
    CB = 256
    for cb in range(D // CB):
        c0 = cb * CB
        ua = jnp.dot(h_ref[...], win_ref[:, c0:c0 + CB], preferred_element_type=F32)
        ug = jnp.dot(h_ref[...], win_ref[:, D + c0:D + c0 + CB], preferred_element_type=F32)
        ua = ua + bin_ref[:, c0:c0 + CB]
        ug = ug + bin_ref[:, D + c0:D + c0 + CB]
        glu = ua * (1.0 / (1.0 + _exp_neg(ug)))
        for k in range(CB // 128):
            gbuf_ref[cb * (CB // 128) + k, HALO:HALO + T, :] = glu[:, k * 128:(k + 1) * 128]

    off = HALO - (KW - 1)
    for lb in range(D // 128):
        l0 = lb * 128

        for rb in range(T // RB):
            r0 = rb * RB
            acc = jnp.broadcast_to(dwb_ref[:, l0:l0 + 128], (RB, 128))
            for j in range(KW):
                acc = acc + dw_ref[j:j + 1, l0:l0 + 128] * _rows_at(gbuf_ref, lb, r0 + off + j, RB)
            cbuf_ref[r0:r0 + RB, l0:l0 + 128] = acc

    for rb in range(T // RB):
        r0 = rb * RB
        c = cbuf_ref[r0:r0 + RB, :]
        mu = jnp.mean(c, axis=-1, keepdims=True)
        cc = c - mu
        var = jnp.mean(cc * cc, axis=-1, keepdims=True)
        y = cc * lax.rsqrt(var + EPS) * lng_ref[...] + lnb_ref[...]
        sbuf_ref[r0:r0 + RB, :] = (y * (1.0 / (1.0 + _exp_neg(y)))).astype(BF16)

    y = jnp.dot(sbuf_ref[...], wout_ref[...], preferred_element_type=F32) + bout_ref[...]
    o_ref[...] = x_ref[...] + _rms(y, gpost_ref[...])


def _conformer(x, gpre, w_in, b_in, dw, dwb, lng, lnb, w_out, b_out, gpost):
    b, s, _ = x.shape
    dwp = jnp.concatenate([dw, jnp.zeros((32 - KW, D), F32)], axis=0)
    n_sub = 4
    return pl.pallas_call(
        functools.partial(_conf_kernel, n_sub=n_sub),
        out_shape=jax.ShapeDtypeStruct(x.shape, F32),
        grid=(b, s // (n_sub * T)),
        in_specs=[
            pl.BlockSpec((None, n_sub * T, D), lambda i, j: (i, j, 0)),
            _const_spec((1, D)),
            _const_spec((D, 2 * D)),
            _const_spec((1, 2 * D)),
            _const_spec((32, D)),
            _const_spec((1, D)),
            _const_spec((1, D)),
            _const_spec((1, D)),
            _const_spec((D, D)),
            _const_spec((1, D)),
            _const_spec((1, D)),
        ],
        out_specs=pl.BlockSpec((None, n_sub * T, D), lambda i, j: (i, j, 0)),
        scratch_shapes=[
            pltpu.VMEM((T, D), BF16),
            pltpu.VMEM((D // 128, T + HALO, 128), F32),
            pltpu.VMEM((T, D), F32),
            pltpu.VMEM((T, D), BF16),
        ],
        compiler_params=pltpu.CompilerParams(
            dimension_semantics=("arbitrary", "arbitrary"),
            vmem_limit_bytes=VMEM_LIMIT),
        name="conformer",
    )(x, gpre.reshape(1, D), w_in.astype(BF16), b_in.reshape(1, 2 * D), dwp, dwb.reshape(1, D),
      lng.reshape(1, D), lnb.reshape(1, D), w_out.astype(BF16), b_out.reshape(1, D),
      gpost.reshape(1, D))


NB = 512


def _proj_kernel(x_ref, g_ref, *refs, n_sub):
    nbr = len(DILS)
    w_refs, o_refs = refs[:nbr], refs[nbr:2 * nbr]
    scratch = refs[2 * nbr:]

    def body(k, carry):
        xs = x_ref.at[pl.ds(pl.multiple_of(k * T, T), T), :]
        outs = [o.at[:, pl.ds(pl.multiple_of(k * (T // dil), T // dil), T // dil), :]
                for o, dil in zip(o_refs, DILS)]
        _proj_tile(xs, g_ref, w_refs, outs, *scratch)
        return carry

    lax.fori_loop(0, n_sub, body, 0)


def _proj_tile(x_ref, g_ref, w_refs, o_refs, h_ref, slab_ref, tmp_ref):
    h_ref[...] = _rms(x_ref[...], g_ref[...]).astype(BF16)
    for w_ref, o_ref, dil in zip(w_refs, o_refs, DILS):
        for nb in range(w_ref.shape[1] // NB):
            res = jnp.dot(h_ref[...], w_ref[:, nb * NB:(nb + 1) * NB], preferred_element_type=F32)
            if dil == 1:
                o_ref[0, :, nb * NB:(nb + 1) * NB] = res.astype(BF16)
                continue
            for k in range(NB // 128):
                slab_ref[k] = res[:, k * 128:(k + 1) * 128]
            for k in range(NB // 128):
                c0 = nb * NB + k * 128
                if dil == 4:
                    for r in range(4):
                        o_ref[r, :, c0:c0 + 128] = (
                            slab_ref[k, pl.ds(r, T // 4, stride=4), :].astype(BF16))
                else:
                    for q in range(4):
                        tmp_ref[q] = slab_ref[k, pl.ds(q, T // 4, stride=4), :]
                    for q in range(4):
                        for p in range(4):
                            o_ref[q + 4 * p, :, c0:c0 + 128] = (
                                tmp_ref[q, pl.ds(p, T // 16, stride=4), :].astype(BF16))


def _proj(x, g, ws):
    assert DILS == (1, 4, 16)
    b, s, _ = x.shape
    n = ws[0].shape[1]
    n_sub = 2
    tb = n_sub * T
    return pl.pallas_call(
        functools.partial(_proj_kernel, n_sub=n_sub),
        out_shape=[jax.ShapeDtypeStruct((b, dil, s // dil, n), BF16) for dil in DILS],
        grid=(b, s // tb),
        in_specs=[pl.BlockSpec((None, tb, D), lambda i, j: (i, j, 0)), _const_spec((1, D))]
        + [_const_spec((D, n))] * len(DILS),
        out_specs=[pl.BlockSpec((None, dil, tb // dil, n), lambda i, j: (i, 0, j, 0))
                   for dil in DILS],
        scratch_shapes=[pltpu.VMEM((T, D), BF16), pltpu.VMEM((NB // 128, T, 128), F32),
                        pltpu.VMEM((4, T // 4, 128), F32)],
        compiler_params=pltpu.CompilerParams(
            dimension_semantics=("arbitrary", "arbitrary"),
            vmem_limit_bytes=VMEM_LIMIT),
        name=f"proj_n{n}",
    )(x, g.reshape(1, D), *ws)


def _attn_kernel(q_ref, kv_ref, o_ref, lse_ref, *, dil, group_size):
    nq = q_ref.shape[1] // BLK

    def token_rows(i, n):
        if dil == 1:
            return pl.ds(n * BLK, BLK)
        return pl.ds(n * BLK * dil + i, BLK, stride=dil)

    row = lax.broadcasted_iota(jnp.int32, (BLK, BLK), 0)
    col = lax.broadcasted_iota(jnp.int32, (BLK, BLK), 1)
    bias_cur = jnp.where(col <= row, 0.0, NEG)
    bias_prev = jnp.where(col >= row, 0.0, NEG)
    scale = 1.0 / math.sqrt(DH)
    exp2_scale = scale * math.log2(math.e)
    dn = (((1,), (1,)), ((), ()))

    def key_rows(n):
        return slice(max(n - 1, 0) * BLK, (n + 1) * BLK)

    def scores(i, n, h):
        q = q_ref[i, n * BLK:(n + 1) * BLK, h * DH:(h + 1) * DH]
        k = kv_ref[i, key_rows(n), h * DH:(h + 1) * DH]
        s = lax.dot_general(q, k, dn, preferred_element_type=F32)
        return s + (jnp.concatenate([bias_prev, bias_cur], axis=1) if n > 0 else bias_cur)

    def weighted_values(i, n, h, p):
        return jnp.dot(p, kv_ref[i, key_rows(n), D + h * DH:D + (h + 1) * DH],
                       preferred_element_type=F32)

    items = [(i, n, h) for i in range(dil) for n in range(nq) for h in range(H)]
    lse_tiles = {}
    for g0 in range(0, len(items), group_size):
        group = items[g0:g0 + group_size]
        s = [scores(*it) for it in group]
        m = [jnp.max(x, axis=-1, keepdims=True) for x in s]
        p = [jnp.exp2((x - mx) * exp2_scale) for x, mx in zip(s, m)]
        den = [jnp.sum(x, axis=-1, keepdims=True) for x in p]
        out = [weighted_values(*it, x.astype(BF16)) for it, x in zip(group, p)]
        for (i, n, h), o, dd, mx in zip(group, out, den, m):
            o_ref[h, token_rows(i, n), :] = o / dd
            tile = lse_tiles.get((i, n), jnp.zeros((BLK, BLK), F32))
            tile = jnp.where(col == h, mx * scale + jnp.log(dd), tile)
            if h == H - 1:
                lse_ref[token_rows(i, n), :] = tile
            else:
                lse_tiles[(i, n)] = tile


def _attn(q, kv, dil):
    b, _, L, _ = q.shape
    s = L * dil
    assert L % BLK == 0
    return pl.pallas_call(
        functools.partial(_attn_kernel, dil=dil, group_size=2 * H if dil == 16 else H),
        out_shape=(jax.ShapeDtypeStruct((b, H, s, DH), F32),
                   jax.ShapeDtypeStruct((b, s, BLK), F32)),
        grid=(b,),
        in_specs=[pl.BlockSpec((None, dil, L, D), lambda i: (i, 0, 0, 0)),
                  pl.BlockSpec((None, dil, L, 2 * D), lambda i: (i, 0, 0, 0))],
        out_specs=(pl.BlockSpec((None, H, s, DH), lambda i: (i, 0, 0, 0)),
                   pl.BlockSpec((None, s, BLK), lambda i: (i, 0, 0))),
        compiler_params=pltpu.CompilerParams(
            dimension_semantics=("arbitrary",),
            vmem_limit_bytes=VMEM_LIMIT),
        name=f"attn_d{dil}",
    )(q, kv)


def _merge_kernel(x_ref, o0_ref, o1_ref, o2_ref, l0_ref, l1_ref, l2_ref, wo_ref, gpost_ref,
                  out_ref, mbuf_ref, *, n_sub):
    def body(k, carry):
        rows = pl.ds(pl.multiple_of(k * T, T), T)
        _merge_tile(x_ref.at[rows, :], [o.at[:, rows, :] for o in (o0_ref, o1_ref, o2_ref)],
                    [l.at[rows, :] for l in (l0_ref, l1_ref, l2_ref)], wo_ref, gpost_ref,
                    out_ref.at[rows, :], mbuf_ref)
        return carry

    lax.fori_loop(0, n_sub, body, 0)


def _merge_tile(x_ref, o_refs, l_refs, wo_ref, gpost_ref, out_ref, mbuf_ref):
    for rb in range(T // RB):
        r0 = rb * RB
        ls = [l[r0:r0 + RB, :] for l in l_refs]
        mx = jnp.maximum(jnp.maximum(ls[0], ls[1]), ls[2])
        es = [jnp.exp(l - mx) for l in ls]
        tot = es[0] + es[1] + es[2]
        ws = [e / tot for e in es]
        for h in range(H):
            c0 = h * DH
            acc = ws[0][:, h:h + 1] * o_refs[0][h, r0:r0 + RB, :]
            acc = acc + ws[1][:, h:h + 1] * o_refs[1][h, r0:r0 + RB, :]
            acc = acc + ws[2][:, h:h + 1] * o_refs[2][h, r0:r0 + RB, :]
            mbuf_ref[r0:r0 + RB, c0:c0 + DH] = acc.astype(BF16)
    y = jnp.dot(mbuf_ref[...], wo_ref[...], preferred_element_type=F32)
    out_ref[...] = x_ref[...] + _rms(y, gpost_ref[...])


def _merge(x, outs, lses, w_o, gpost):
    b, s, _ = x.shape
    n_sub = 2
    tb = n_sub * T
    tile = lambda w: pl.BlockSpec((None, tb, w), lambda i, j: (i, j, 0))
    heads = pl.BlockSpec((None, H, tb, DH), lambda i, j: (i, 0, j, 0))
    return pl.pallas_call(
        functools.partial(_merge_kernel, n_sub=n_sub),
        out_shape=jax.ShapeDtypeStruct(x.shape, F32),
        grid=(b, s // tb),
        in_specs=[tile(D)] + [heads] * 3 + [tile(BLK)] * 3 + [_const_spec((D, D)), _const_spec((1, D))],
        out_specs=tile(D),
        scratch_shapes=[pltpu.VMEM((T, D), BF16)],
        compiler_params=pltpu.CompilerParams(
            dimension_semantics=("arbitrary", "arbitrary"),
            vmem_limit_bytes=VMEM_LIMIT),
        name="merge",
    )(x, *outs, *lses, w_o.astype(BF16), gpost.reshape(1, D))


def kernel(x, mix_pre_g, mix_post_g, ffn_pre_g, ffn_post_g, cm_w_in, cm_b_in, cm_dw, cm_dw_b,
           cm_ln_g, cm_ln_b, cm_w_out, cm_b_out, kv_norm_g, w_kv, w_q, w_o, ffn_w_in, ffn_dw,
           ffn_dw_b, ffn_w_out):
    n_a = cm_w_in.shape[0]
    depth = ffn_w_in.shape[0]
    qw = len(DILS) * D
    kvs = None
    for i in range(depth):
        if i < n_a:
            x = _conformer(x, mix_pre_g[i], cm_w_in[i], cm_b_in[i], cm_dw[i], cm_dw_b[i],
                           cm_ln_g[i], cm_ln_b[i], cm_w_out[i], cm_b_out[i], mix_post_g[i])
        else:
            j = i - n_a
            qs = _proj(x, mix_pre_g[i],
                       [w_q[j][:, g * D:(g + 1) * D].astype(BF16) for g in range(len(DILS))])
            outs, lses = [], []
            for g, dil in enumerate(DILS):
                o, l = _attn(qs[g], kvs[g], dil)
                outs.append(o)
                lses.append(l)
            x = _merge(x, outs, lses, w_o[j], mix_post_g[i])
        x = _ffn(x, ffn_pre_g[i], ffn_w_in[i], ffn_dw[i], ffn_dw_b[i], ffn_w_out[i], ffn_post_g[i])
        if i == n_a - 1:
            kvs = _proj(x, kv_norm_g, [
                jnp.concatenate([w_kv[:, g * D:(g + 1) * D],
                                 w_kv[:, qw + g * D:qw + (g + 1) * D]], axis=1).astype(BF16)
                for g in range(len(DILS))])
    return x
```

```python
import functools
import math

import jax
import jax.numpy as jnp
from jax import lax
from jax.experimental import pallas as pl
from jax.experimental.pallas import tpu as pltpu

D = 1024
F = 2816
H = 8
DH = 128
DILS = (1, 4, 16)
BLK = 128
KW = 31
EPS = 1e-6
NEG = -1e30

T = 512
FC = 256
NC = F // FC
NS = 2 * FC // 128
HALO = 32
RB = 64
GRB = 64
VMEM_LIMIT = 56 * 1024 * 1024

F32 = jnp.float32
BF16 = jnp.bfloat16


def _exp_neg(x):
    return jnp.exp2(x * (-math.log2(math.e)))


def _rms(x, g):
    return x * lax.rsqrt(jnp.mean(x * x, axis=-1, keepdims=True) + EPS) * g


def _rows_at(ref, slab, start, rows):
    return ref[pl.ds(slab, 1, stride=2), pl.ds(start, rows), :][0]


def _const_spec(shape):
    n = len(shape)
    if math.prod(shape) < 256 * 1024:
        return pl.BlockSpec(shape, lambda *_: (0,) * n)
    return pl.BlockSpec(shape, lambda *_: (0,) * n, pipeline_mode=pl.Buffered(1))


def _sub_tiles(n_sub, tile_fn, row_refs, *rest):
    def body(k, carry):
        rows = pl.ds(pl.multiple_of(k * T, T), T)
        tile_fn(*[r.at[rows, :] for r in row_refs], *rest)
        return carry
    lax.fori_loop(0, n_sub, body, 0)


def _ffn_kernel(x_ref, gpre_ref, win_ref, dwc_ref, wout_ref, gpost_ref, o_ref,
                h_ref, ubuf_ref, act_ref, y_ref, tail_ref, *, n_sub):
    @pl.when(pl.program_id(1) == 0)
    def _():
        tail_ref[...] = jnp.zeros_like(tail_ref)

    _sub_tiles(n_sub, _ffn_tile, (x_ref, o_ref), gpre_ref, win_ref, dwc_ref, wout_ref, gpost_ref,
               h_ref, ubuf_ref, act_ref, y_ref, tail_ref)


def _ffn_tile(x_ref, o_ref, gpre_ref, win_ref, dwc_ref, wout_ref, gpost_ref,
              h_ref, ubuf_ref, act_ref, y_ref, tail_ref):
    for rb in range(T // RB):
        rows = slice(rb * RB, (rb + 1) * RB)
        h_ref[rows, :] = _rms(x_ref[rows, :], gpre_ref[...]).astype(BF16)

    def up_half(c, slot, half):
        n0 = half * F + c * FC
        u = jnp.dot(h_ref[...], win_ref[:, n0:n0 + FC], preferred_element_type=F32)
        for k in range(FC // 128):
            sl = slot * NS + half * (FC // 128) + k
            ubuf_ref[sl, 0:8, :] = tail_ref[:, n0 + k * 128:n0 + (k + 1) * 128]
            ubuf_ref[sl, 8:8 + T, :] = u[:, k * 128:(k + 1) * 128]
        tail_ref[:, n0:n0 + FC] = u[T - 8:T, :]

    def gate_rows(c, slot, r0):
        def conv(sl, l0):
            w = dwc_ref[:, l0:l0 + 128]
            return (w[0:1] * _rows_at(ubuf_ref, sl, 6 + r0, GRB)
                    + w[1:2] * _rows_at(ubuf_ref, sl, 7 + r0, GRB)
                    + w[2:3] * ubuf_ref[sl, 8 + r0:8 + r0 + GRB, :]) + w[3:4]

        for k in range(FC // 128):
            a = conv(slot * NS + k, c * FC + k * 128)
            g = conv(slot * NS + FC // 128 + k, F + c * FC + k * 128)
            act_ref[r0:r0 + GRB, c * FC + k * 128:c * FC + (k + 1) * 128] = (
                (g / (1.0 + _exp_neg(g))) * a).astype(BF16)

    up_half(0, 0, 0)
    up_half(0, 0, 1)
    for c in range(NC):
        slot = c % 2
        if c + 1 < NC:
            up_half(c + 1, 1 - slot, 0)
            up_half(c + 1, 1 - slot, 1)
        for rb in range(T // GRB):
            gate_rows(c, slot, rb * GRB)
    y_ref[...] = jnp.dot(act_ref[...], wout_ref[...], preferred_element_type=F32)
    for rb in range(T // RB):
        rows = slice(rb * RB, (rb + 1) * RB)
        o_ref[rows, :] = x_ref[rows, :] + _rms(y_ref[rows, :], gpost_ref[...])


def _ffn(x, gpre, w_in, dw, dwb, w_out, gpost):
    b, s, _ = x.shape
    win_c = w_in.astype(BF16)
    dwc = jnp.concatenate([dw, dwb[None, :], jnp.zeros((4, 2 * F), F32)], axis=0)
    wout_c = w_out.astype(BF16)
    n_sub = 2
    return pl.pallas_call(
        functools.partial(_ffn_kernel, n_sub=n_sub),
        out_shape=jax.ShapeDtypeStruct(x.shape, F32),
        grid=(b, s // (n_sub * T)),
        in_specs=[
            pl.BlockSpec((None, n_sub * T, D), lambda i, j: (i, j, 0)),
            _const_spec((1, D)),
            _const_spec((D, 2 * F)),
            _const_spec((8, 2 * F)),
            _const_spec((F, D)),
            _const_spec((1, D)),
        ],
        out_specs=pl.BlockSpec((None, n_sub * T, D), lambda i, j: (i, j, 0)),
        scratch_shapes=[
            pltpu.VMEM((T, D), BF16),
            pltpu.VMEM((2 * NS, T + 8, 128), F32),
            pltpu.VMEM((T, F), BF16),
            pltpu.VMEM((T, D), F32),
            pltpu.VMEM((8, 2 * F), F32),
        ],
        compiler_params=pltpu.CompilerParams(
            dimension_semantics=("arbitrary", "arbitrary"),
            vmem_limit_bytes=VMEM_LIMIT),
        name="conv_ffn",
    )(x, gpre.reshape(1, D), win_c, dwc, wout_c, gpost.reshape(1, D))


def _conf_kernel(x_ref, gpre_ref, win_ref, bin_ref, dw_ref, dwb_ref, lng_ref, lnb_ref,
                 wout_ref, bout_ref, gpost_ref, o_ref, h_ref, gbuf_ref, cbuf_ref, sbuf_ref,
                 *, n_sub):
    @pl.when(pl.program_id(1) == 0)
    def _():
        gbuf_ref[:, T:T + HALO, :] = jnp.zeros((D // 128, HALO, 128), F32)

    _sub_tiles(n_sub, _conf_tile, (x_ref, o_ref), gpre_ref, win_ref, bin_ref, dw_ref, dwb_ref,
               lng_ref, lnb_ref, wout_ref, bout_ref, gpost_ref, h_ref, gbuf_ref, cbuf_ref, sbuf_ref)


def _conf_tile(x_ref, o_ref, gpre_ref, win_ref, bin_ref, dw_ref, dwb_ref, lng_ref, lnb_ref,
               wout_ref, bout_ref, gpost_ref, h_ref, gbuf_ref, cbuf_ref, sbuf_ref):
    gbuf_ref[:, 0:HALO, :] = gbuf_ref[:, T:T + HALO, :]
    h_ref[...] = _rms(x_ref[...], gpre_ref[...]).astype(BF16)
    CB = 256
    for cb in range(D // CB):
        c0 = cb * CB
        ua = jnp.dot(h_ref[...], win_ref[:, c0:c0 + CB], preferred_element_type=F32)
        ug = jnp.dot(h_ref[...], win_ref[:, D + c0:D + c0 + CB], preferred_element_type=F32)
        ua = ua + bin_ref[:, c0:c0 + CB]
        ug = ug + bin_ref[:, D + c0:D + c0 + CB]
        glu = ua * (1.0 / (1.0 + _exp_neg(ug)))
        for k in range(CB // 128):
            gbuf_ref[cb * (CB // 128) + k, HALO:HALO + T, :] = glu[:, k * 128:(k + 1) * 128]

    off = HALO - (KW - 1)
    for lb in range(D // 128):
        l0 = lb * 128

        for rb in range(T // RB):
            r0 = rb * RB
            acc = jnp.broadcast_to(dwb_ref[:, l0:l0 + 128], (RB, 128))
            for j in range(KW):
                acc = acc + dw_ref[j:j + 1, l0:l0 + 128] * _rows_at(gbuf_ref, lb, r0 + off + j, RB)
            cbuf_ref[r0:r0 + RB, l0:l0 + 128] = acc

    for rb in range(T // RB):
        r0 = rb * RB
        c = cbuf_ref[r0:r0 + RB, :]
        mu = jnp.mean(c, axis=-1, keepdims=True)
        cc = c - mu
        var = jnp.mean(cc * cc, axis=-1, keepdims=True)
        y = cc * lax.rsqrt(var + EPS) * lng_ref[...] + lnb_ref[...]
        sbuf_ref[r0:r0 + RB, :] = (y * (1.0 / (1.0 + _exp_neg(y)))).astype(BF16)

    y = jnp.dot(sbuf_ref[...], wout_ref[...], preferred_element_type=F32) + bout_ref[...]
    o_ref[...] = x_ref[...] + _rms(y, gpost_ref[...])


def _conformer(x, gpre, w_in, b_in, dw, dwb, lng, lnb, w_out, b_out, gpost):
    b, s, _ = x.shape
    dwp = jnp.concatenate([dw, jnp.zeros((32 - KW, D), F32)], axis=0)
    n_sub = 4
    return pl.pallas_call(
        functools.partial(_conf_kernel, n_sub=n_sub),
        out_shape=jax.ShapeDtypeStruct(x.shape, F32),
        grid=(b, s // (n_sub * T)),
        in_specs=[
            pl.BlockSpec((None, n_sub * T, D), lambda i, j: (i, j, 0)),
            _const_spec((1, D)),
            _const_spec((D, 2 * D)),
            _const_spec((1, 2 * D)),
            _const_spec((32, D)),
            _const_spec((1, D)),
            _const_spec((1, D)),
            _const_spec((1, D)),
            _const_spec((D, D)),
            _const_spec((1, D)),
            _const_spec((1, D)),
        ],
        out_specs=pl.BlockSpec((None, n_sub * T, D), lambda i, j: (i, j, 0)),
        scratch_shapes=[
            pltpu.VMEM((T, D), BF16),
            pltpu.VMEM((D // 128, T + HALO, 128), F32),
            pltpu.VMEM((T, D), F32),
            pltpu.VMEM((T, D), BF16),
        ],
        compiler_params=pltpu.CompilerParams(
            dimension_semantics=("arbitrary", "arbitrary"),
            vmem_limit_bytes=VMEM_LIMIT),
        name="conformer",
    )(x, gpre.reshape(1, D), w_in.astype(BF16), b_in.reshape(1, 2 * D), dwp, dwb.reshape(1, D),
      lng.reshape(1, D), lnb.reshape(1, D), w_out.astype(BF16), b_out.reshape(1, D),
      gpost.reshape(1, D))


NB = 512


def _proj_kernel(x_ref, g_ref, *refs, n_sub):
    nbr = len(DILS)
    w_refs, o_refs = refs[:nbr], refs[nbr:2 * nbr]
    scratch = refs[2 * nbr:]

    def body(k, carry):
        xs = x_ref.at[pl.ds(pl.multiple_of(k * T, T), T), :]
        outs = [o.at[:, pl.ds(pl.multiple_of(k * (T // dil), T // dil), T // dil), :]
                for o, dil in zip(o_refs, DILS)]
        _proj_tile(xs, g_ref, w_refs, outs, *scratch)
        return carry

    lax.fori_loop(0, n_sub, body, 0)


def _proj_tile(x_ref, g_ref, w_refs, o_refs, h_ref, slab_ref, tmp_ref):
    h_ref[...] = _rms(x_ref[...], g_ref[...]).astype(BF16)
    for w_ref, o_ref, dil in zip(w_refs, o_refs, DILS):
        for nb in range(w_ref.shape[1] // NB):
            res = jnp.dot(h_ref[...], w_ref[:, nb * NB:(nb + 1) * NB], preferred_element_type=F32)
            if dil == 1:
                o_ref[0, :, nb * NB:(nb + 1) * NB] = res.astype(BF16)
                continue
            for k in range(NB // 128):
                slab_ref[k] = res[:, k * 128:(k + 1) * 128]
            for k in range(NB // 128):
                c0 = nb * NB + k * 128
                if dil == 4:
                    for r in range(4):
                        o_ref[r, :, c0:c0 + 128] = (
                            slab_ref[k, pl.ds(r, T // 4, stride=4), :].astype(BF16))
                else:
                    for q in range(4):
                        tmp_ref[q] = slab_ref[k, pl.ds(q, T // 4, stride=4), :]
                    for q in range(4):
                        for p in range(4):
                            o_ref[q + 4 * p, :, c0:c0 + 128] = (
                                tmp_ref[q, pl.ds(p, T // 16, stride=4), :].astype(BF16))


def _proj(x, g, ws):
    assert DILS == (1, 4, 16)
    b, s, _ = x.shape
    n = ws[0].shape[1]
    n_sub = 2
    tb = n_sub * T
    return pl.pallas_call(
        functools.partial(_proj_kernel, n_sub=n_sub),
        out_shape=[jax.ShapeDtypeStruct((b, dil, s // dil, n), BF16) for dil in DILS],
        grid=(b, s // tb),
        in_specs=[pl.BlockSpec((None, tb, D), lambda i, j: (i, j, 0)), _const_spec((1, D))]
        + [_const_spec((D, n))] * len(DILS),
        out_specs=[pl.BlockSpec((None, dil, tb // dil, n), lambda i, j: (i, 0, j, 0))
                   for dil in DILS],
        scratch_shapes=[pltpu.VMEM((T, D), BF16), pltpu.VMEM((NB // 128, T, 128), F32),
                        pltpu.VMEM((4, T // 4, 128), F32)],
        compiler_params=pltpu.CompilerParams(
            dimension_semantics=("arbitrary", "arbitrary"),
            vmem_limit_bytes=VMEM_LIMIT),
        name=f"proj_n{n}",
    )(x, g.reshape(1, D), *ws)


def _attn_kernel(q_ref, kv_ref, o_ref, lse_ref, *, dil, group_size):
    nq = q_ref.shape[1] // BLK

    def token_rows(i, n):
        if dil == 1:
            return pl.ds(n * BLK, BLK)
        return pl.ds(n * BLK * dil + i, BLK, stride=dil)

    row = lax.broadcasted_iota(jnp.int32, (BLK, BLK), 0)
    col = lax.broadcasted_iota(jnp.int32, (BLK, BLK), 1)
    bias_cur = jnp.where(col <= row, 0.0, NEG)
    bias_prev = jnp.where(col >= row, 0.0, NEG)
    scale = 1.0 / math.sqrt(DH)
    exp2_scale = scale * math.log2(math.e)
    dn = (((1,), (1,)), ((), ()))

    def key_rows(n):
        return slice(max(n - 1, 0) * BLK, (n + 1) * BLK)

    def scores(i, n, h):
        q = q_ref[i, n * BLK:(n + 1) * BLK, h * DH:(h + 1) * DH]
        k = kv_ref[i, key_rows(n), h * DH:(h + 1) * DH]
        s = lax.dot_general(q, k, dn, preferred_element_type=F32)
        return s + (jnp.concatenate([bias_prev, bias_cur], axis=1) if n > 0 else bias_cur)

    def weighted_values(i, n, h, p):
        return jnp.dot(p, kv_ref[i, key_rows(n), D + h * DH:D + (h + 1) * DH],
                       preferred_element_type=F32)

    items = [(i, n, h) for i in range(dil) for n in range(nq) for h in range(H)]
    lse_tiles = {}
    for g0 in range(0, len(items), group_size):
        group = items[g0:g0 + group_size]
        s = [scores(*it) for it in group]
        m = [jnp.max(x, axis=-1, keepdims=True) for x in s]
        p = [jnp.exp2((x - mx) * exp2_scale) for x, mx in zip(s, m)]
        den = [jnp.sum(x, axis=-1, keepdims=True) for x in p]
        out = [weighted_values(*it, x.astype(BF16)) for it, x in zip(group, p)]
        for (i, n, h), o, dd, mx in zip(group, out, den, m):
            o_ref[h, token_rows(i, n), :] = (o / dd).astype(o_ref.dtype)
            tile = lse_tiles.get((i, n), jnp.zeros((BLK, BLK), F32))
            tile = jnp.where(col == h, mx * scale + jnp.log(dd), tile)
            if h == H - 1:
                lse_ref[token_rows(i, n), :] = tile
            else:
                lse_tiles[(i, n)] = tile


def _attn(q, kv, dil):
    b, _, L, _ = q.shape
    s = L * dil
    assert L % BLK == 0
    return pl.pallas_call(
        functools.partial(_attn_kernel, dil=dil, group_size=2 * H if dil == 16 else H),
        out_shape=(jax.ShapeDtypeStruct((b, H, s, DH), BF16 if dil == 1 else F32),
                   jax.ShapeDtypeStruct((b, s, BLK), F32)),
        grid=(b,),
        in_specs=[pl.BlockSpec((None, dil, L, D), lambda i: (i, 0, 0, 0)),
                  pl.BlockSpec((None, dil, L, 2 * D), lambda i: (i, 0, 0, 0))],
        out_specs=(pl.BlockSpec((None, H, s, DH), lambda i: (i, 0, 0, 0)),
                   pl.BlockSpec((None, s, BLK), lambda i: (i, 0, 0))),
        compiler_params=pltpu.CompilerParams(
            dimension_semantics=("arbitrary",),
            vmem_limit_bytes=VMEM_LIMIT),
        name=f"attn_d{dil}",
    )(q, kv)


def _merge_kernel(x_ref, o0_ref, o1_ref, o2_ref, l0_ref, l1_ref, l2_ref, wo_ref, gpost_ref,
                  out_ref, mbuf_ref, *, n_sub):
    def body(k, carry):
        rows = pl.ds(pl.multiple_of(k * T, T), T)
        _merge_tile(x_ref.at[rows, :], [o.at[:, rows, :] for o in (o0_ref, o1_ref, o2_ref)],
                    [l.at[rows, :] for l in (l0_ref, l1_ref, l2_ref)], wo_ref, gpost_ref,
                    out_ref.at[rows, :], mbuf_ref)
        return carry

    lax.fori_loop(0, n_sub, body, 0)


def _merge_tile(x_ref, o_refs, l_refs, wo_ref, gpost_ref, out_ref, mbuf_ref):
    for rb in range(T // RB):
        r0 = rb * RB
        ls = [l[r0:r0 + RB, :] for l in l_refs]
        mx = jnp.maximum(jnp.maximum(ls[0], ls[1]), ls[2])
        es = [jnp.exp(l - mx) for l in ls]
        tot = es[0] + es[1] + es[2]
        ws = [e / tot for e in es]
        for h in range(H):
            c0 = h * DH
            acc = ws[0][:, h:h + 1] * o_refs[0][h, r0:r0 + RB, :].astype(F32)
            acc = acc + ws[1][:, h:h + 1] * o_refs[1][h, r0:r0 + RB, :]
            acc = acc + ws[2][:, h:h + 1] * o_refs[2][h, r0:r0 + RB, :]
            mbuf_ref[r0:r0 + RB, c0:c0 + DH] = acc.astype(BF16)
    y = jnp.dot(mbuf_ref[...], wo_ref[...], preferred_element_type=F32)
    out_ref[...] = x_ref[...] + _rms(y, gpost_ref[...])


def _merge(x, outs, lses, w_o, gpost):
    b, s, _ = x.shape
    n_sub = 2
    tb = n_sub * T
    tile = lambda w: pl.BlockSpec((None, tb, w), lambda i, j: (i, j, 0))
    heads = pl.BlockSpec((None, H, tb, DH), lambda i, j: (i, 0, j, 0))
    return pl.pallas_call(
        functools.partial(_merge_kernel, n_sub=n_sub),
        out_shape=jax.ShapeDtypeStruct(x.shape, F32),
        grid=(b, s // tb),
        in_specs=[tile(D)] + [heads] * 3 + [tile(BLK)] * 3 + [_const_spec((D, D)), _const_spec((1, D))],
        out_specs=tile(D),
        scratch_shapes=[pltpu.VMEM((T, D), BF16)],
        compiler_params=pltpu.CompilerParams(
            dimension_semantics=("arbitrary", "arbitrary"),
            vmem_limit_bytes=VMEM_LIMIT),
        name="merge",
    )(x, *outs, *lses, w_o.astype(BF16), gpost.reshape(1, D))


def kernel(x, mix_pre_g, mix_post_g, ffn_pre_g, ffn_post_g, cm_w_in, cm_b_in, cm_dw, cm_dw_b,
           cm_ln_g, cm_ln_b, cm_w_out, cm_b_out, kv_norm_g, w_kv, w_q, w_o, ffn_w_in, ffn_dw,
           ffn_dw_b, ffn_w_out):
    n_a = cm_w_in.shape[0]
    depth = ffn_w_in.shape[0]
    qw = len(DILS) * D
    kvs = None
    for i in range(depth):
        if i < n_a:
            x = _conformer(x, mix_pre_g[i], cm_w_in[i], cm_b_in[i], cm_dw[i], cm_dw_b[i],
                           cm_ln_g[i], cm_ln_b[i], cm_w_out[i], cm_b_out[i], mix_post_g[i])
        else:
            j = i - n_a
            qs = _proj(x, mix_pre_g[i],
                       [w_q[j][:, g * D:(g + 1) * D].astype(BF16) for g in range(len(DILS))])
            outs, lses = [], []
            for g, dil in enumerate(DILS):
                o, l = _attn(qs[g], kvs[g], dil)
                outs.append(o)
                lses.append(l)
            x = _merge(x, outs, lses, w_o[j], mix_post_g[i])
        x = _ffn(x, ffn_pre_g[i], ffn_w_in[i], ffn_dw[i], ffn_dw_b[i], ffn_w_out[i], ffn_post_g[i])
        if i == n_a - 1:
            kvs = _proj(x, kv_norm_g, [
                jnp.concatenate([w_kv[:, g * D:(g + 1) * D],
                                 w_kv[:, qw + g * D:qw + (g + 1) * D]], axis=1).astype(BF16)
                for g in range(len(DILS))])
    return x
```
